```python
import math
import jax, jax.numpy as jnp
from jax import lax
import numpy as np

D_MODEL = 2048
BATCH = 32
SEQ = 256
DEPTH = 4
DEC_BATCH = 2
DEC_SEQ = 1024
PAST_LEN = 512

GRID_W = 64
N_EVEN = (DEPTH + 1) // 2
N_ODD = DEPTH // 2
EPS = 1e-6
ROPE_BASE = 10000.0
Q_BLOCK = 128
A_HEADS = 8
A_DK = 64
A_DV = 2 * A_DK
A_QK = A_HEADS * 2 * A_DK
A_WIDTH = A_HEADS * A_DV
B_WIDTH = D_MODEL // 2
B_KERNEL = 31
B_PAD = (B_KERNEL - 1) // 2
C_WIDTH = D_MODEL // 2
C_GROUP = 16
C_GROUPS = C_WIDTH // C_GROUP
C_STATE = 64
D_HEADS = 8
D_KV_HEADS = 2
D_REP = D_HEADS // D_KV_HEADS
D_HEAD_DIM = 128
D_WIDTH = D_HEADS * D_HEAD_DIM
D_KV_WIDTH = D_KV_HEADS * D_HEAD_DIM
EVEN_IN = 2 * A_QK + A_WIDTH + 2 * B_WIDTH
ODD_IN = C_WIDTH + D_WIDTH + 2 * D_KV_WIDTH
D_FF = 4 * D_MODEL
N_MOD = 6

kernel_name = "hybrid_diffusion_prefix_trunk_step"

F32 = jnp.float32


def rms_norm(x, g):
    xf = x.astype(F32)
    y = xf * lax.rsqrt(jnp.mean(xf * xf, axis=-1, keepdims=True) + EPS)
    return (y * g.astype(F32)).astype(x.dtype)


def layer_norm(x, g, b):
    xf = x.astype(F32)
    mu = jnp.mean(xf, axis=-1, keepdims=True)
    xc = xf - mu
    var = jnp.mean(xc * xc, axis=-1, keepdims=True)
    return (xc * lax.rsqrt(var + EPS) * g.astype(F32) + b.astype(F32)).astype(x.dtype)


def ada_modulation(cond, w, bias):
    m = jax.nn.silu(cond.astype(F32)).astype(w.dtype) @ w + bias
    return jnp.split(m[:, None, :], N_MOD, axis=-1)


def modulate(x, g, shift, scale):
    return rms_norm(x, g) * (1.0 + scale) + shift


def axial_rope(n_tokens, dim):
    rows = n_tokens // GRID_W
    row = jnp.repeat(jnp.arange(rows, dtype=F32), GRID_W)
    col = jnp.tile(jnp.arange(GRID_W, dtype=F32), rows)
    quarter = dim // 4
    inv_freq = ROPE_BASE ** (-jnp.arange(quarter, dtype=F32) / quarter)
    ang_r = row[:, None] * inv_freq[None, :]
    ang_c = col[:, None] * inv_freq[None, :]
    ang = jnp.concatenate([ang_r, ang_r, ang_c, ang_c], axis=-1)
    return jnp.cos(ang), jnp.sin(ang)


def apply_rope(x, cos, sin):
    xf = x.astype(F32)
    x1, x2, x3, x4 = jnp.split(xf, 4, axis=-1)
    rot = jnp.concatenate([-x2, x1, -x4, x3], axis=-1)
    return (xf * cos + rot * sin).astype(x.dtype)


def sweep_queries(block_fn, q):
    *lead, t, d = q.shape
    nb = t // Q_BLOCK
    qb = jnp.moveaxis(q.reshape(*lead, nb, Q_BLOCK, d), -3, 0)
    out = jnp.moveaxis(lax.map(block_fn, qb), 0, -3)
    return out.reshape(*lead, t, out.shape[-1])


def diff_attention(q, k, v, lam):
    k1, k2 = k[..., :A_DK], k[..., A_DK:]
    scale = A_DK ** -0.5

    def block(qb):
        q1, q2 = qb[..., :A_DK], qb[..., A_DK:]
        p1 = jax.nn.softmax(jnp.einsum('bhqd,bhkd->bhqk', q1, k1).astype(F32) * scale, axis=-1)
        p2 = jax.nn.softmax(jnp.einsum('bhqd,bhkd->bhqk', q2, k2).astype(F32) * scale, axis=-1)
        p = (p1 - lam * p2).astype(v.dtype)
        return jnp.einsum('bhqk,bhkd->bhqd', p, v)

    return sweep_queries(block, q)


def gqa_attention(q, k, v):
    scale = D_HEAD_DIM ** -0.5

    def block(qb):
        s = jnp.einsum('bgrqd,bgkd->bgrqk', qb, k).astype(F32) * scale
        p = jax.nn.softmax(s, axis=-1).astype(v.dtype)
        return jnp.einsum('bgrqk,bgkd->bgrqd', p, v)

    return sweep_queries(block, q)


def conformer_conv(z, w, bias, ln):
    a, g = jnp.split(z, 2, axis=-1)
    h = a * jax.nn.sigmoid(g)
    h = lax.conv_general_dilated(h, w[:, None, :], window_strides=(1,), padding=[(B_PAD, B_PAD)],
                                 dimension_numbers=('NWC', 'WIO', 'NWC'),
                                 feature_group_count=B_WIDTH) + bias
    h = layer_norm(h, ln[0], ln[1])
    return jax.nn.silu(h)


def s5_discretize(a_re, a_im, log_dt, b_re, b_im):
    dt = jnp.exp(log_dt.astype(F32))[:, None]
    ar = a_re.astype(F32)
    ai = a_im.astype(F32)
    mag = jnp.exp(ar * dt)
    abar_re = mag * jnp.cos(ai * dt)
    abar_im = mag * jnp.sin(ai * dt)
    den = ar * ar + ai * ai
    nr = abar_re - 1.0
    ni = abar_im
    k_re = ((nr * ar + ni * ai) / den)[..., None]
    k_im = ((ni * ar - nr * ai) / den)[..., None]
    br = b_re.astype(F32)
    bi = b_im.astype(F32)
    return abar_re, abar_im, k_re * br - k_im * bi, k_re * bi + k_im * br


def complex_affine_combine(e1, e2):
    a1r, a1i, b1r, b1i = e1
    a2r, a2i, b2r, b2i = e2
    return (a2r * a1r - a2i * a1i,
            a2r * a1i + a2i * a1r,
            a2r * b1r - a2i * b1i + b2r,
            a2r * b1i + a2i * b1r + b2i)


def s5_direction(u, abar_re, abar_im, bbar_re, bbar_im, c_re, c_im, h0_re, h0_im, reverse):
    bu_re = jnp.einsum('btgh,gph->btgp', u, bbar_re)
    bu_im = jnp.einsum('btgh,gph->btgp', u, bbar_im)
    first = -1 if reverse else 0
    bu_re = bu_re.at[:, first].add(abar_re * h0_re - abar_im * h0_im)
    bu_im = bu_im.at[:, first].add(abar_re * h0_im + abar_im * h0_re)
    a_re = jnp.broadcast_to(abar_re, bu_re.shape)
    a_im = jnp.broadcast_to(abar_im, bu_im.shape)
    _, _, h_re, h_im = lax.associative_scan(complex_affine_combine, (a_re, a_im, bu_re, bu_im),
                                            reverse=reverse, axis=1)
    y = (jnp.einsum('gnp,btgp->btgn', c_re.astype(F32), h_re)
         - jnp.einsum('gnp,btgp->btgn', c_im.astype(F32), h_im))
    last = 0 if reverse else -1
    return y, h_re[:, last], h_im[:, last]


def s5_mixer(u, a_re, a_im, log_dt, b, c, d, glu_w, glu_b, h0):
    bsz, t, _ = u.shape
    uf = u.astype(F32).reshape(bsz, t, C_GROUPS, C_GROUP)
    h0 = h0.astype(F32)
    y = d.astype(F32).reshape(C_GROUPS, C_GROUP) * uf
    finals = []
    for direction, rev in enumerate((False, True)):
        abr, abi, bbr, bbi = s5_discretize(a_re[direction], a_im[direction], log_dt[direction],
                                           b[direction, 0], b[direction, 1])
        yd, hr, hi = s5_direction(uf, abr, abi, bbr, bbi, c[direction, 0], c[direction, 1],
                                  h0[:, direction, 0], h0[:, direction, 1], rev)
        y = y + yd
        finals.append(jnp.stack([hr, hi], axis=1))
    state = jnp.stack(finals, axis=1)
    y = jax.nn.gelu(y.reshape(bsz, t, C_WIDTH)).astype(u.dtype)
    return y * jax.nn.sigmoid(y @ glu_w + glu_b), state


def even_mixer(h, lam_init, w_in, w_out, lam_p, subln, conv_w, conv_b, conv_ln, ctx_k, ctx_v, rope):
    bsz, t, _ = h.shape
    proj = h @ w_in
    q, k, v, z = jnp.split(proj, [A_QK, 2 * A_QK, 2 * A_QK + A_WIDTH], axis=-1)
    q = q.reshape(bsz, t, A_HEADS, 2 * A_DK).transpose(0, 2, 1, 3)
    k = k.reshape(bsz, t, A_HEADS, 2 * A_DK).transpose(0, 2, 1, 3)
    v = v.reshape(bsz, t, A_HEADS, A_DV).transpose(0, 2, 1, 3)
    if rope is not None:
        cos, sin = rope
        cos, sin = cos[:, None, :], sin[:, None, :]
        q = apply_rope(q.reshape(bsz, A_HEADS, t, 2, A_DK), cos, sin).reshape(bsz, A_HEADS, t, 2 * A_DK)
        k = apply_rope(k.reshape(bsz, A_HEADS, t, 2, A_DK), cos, sin).reshape(bsz, A_HEADS, t, 2 * A_DK)
    keys = k if ctx_k is None else jnp.concatenate([k, ctx_k.astype(k.dtype)], axis=2)
    vals = v if ctx_v is None else jnp.concatenate([v, ctx_v.astype(v.dtype)], axis=2)
    lp = lam_p.astype(F32)
    lam = jnp.exp(jnp.sum(lp[0] * lp[1])) - jnp.exp(jnp.sum(lp[2] * lp[3])) + lam_init
    att = diff_attention(q, keys, vals, lam)
    att = rms_norm(att, subln) * (1.0 - lam_init)
    att = att.transpose(0, 2, 1, 3).reshape(bsz, t, A_WIDTH)
    conv = conformer_conv(z, conv_w, conv_b, conv_ln).astype(att.dtype)
    out = jnp.concatenate([att, conv], axis=-1) @ w_out
    return out, k, v


def odd_mixer(h, w_in, w_out, a_re, a_im, log_dt, ssm_b, ssm_c, ssm_d, glu_w, glu_b, qk_g,
              ctx_k, ctx_v, h0, rope):
    bsz, t, _ = h.shape
    proj = h @ w_in
    u, q, k, v = jnp.split(proj, [C_WIDTH, C_WIDTH + D_WIDTH, C_WIDTH + D_WIDTH + D_KV_WIDTH], axis=-1)
    ssm_out, state = s5_mixer(u, a_re, a_im, log_dt, ssm_b, ssm_c, ssm_d, glu_w, glu_b, h0)
    q = rms_norm(q.reshape(bsz, t, D_HEADS, D_HEAD_DIM), qk_g[0])
    k = rms_norm(k.reshape(bsz, t, D_KV_HEADS, D_HEAD_DIM), qk_g[1])
    v = v.reshape(bsz, t, D_KV_HEADS, D_HEAD_DIM)
    if rope is not None:
        cos, sin = rope
        q = apply_rope(q, cos[:, None, :], sin[:, None, :])
        k = apply_rope(k, cos[:, None, :], sin[:, None, :])
    q = q.reshape(bsz, t, D_KV_HEADS, D_REP, D_HEAD_DIM).transpose(0, 2, 3, 1, 4)
    k = k.transpose(0, 2, 1, 3)
    v = v.transpose(0, 2, 1, 3)
    keys = k if ctx_k is None else jnp.concatenate([k, ctx_k.astype(k.dtype)], axis=2)
    vals = v if ctx_v is None else jnp.concatenate([v, ctx_v.astype(v.dtype)], axis=2)
    att = gqa_attention(q, keys, vals)
    att = att.transpose(0, 3, 1, 2, 4).reshape(bsz, t, D_WIDTH)
    out = jnp.concatenate([ssm_out.astype(att.dtype), att], axis=-1) @ w_out
    return out, k, v, state


def sq_relu_mlp(h, w1, w2):
    a = jax.nn.relu(h @ w1)
    return (a * a) @ w2


def setup_inputs(seed: int = 0) -> dict:
    key = jax.random.key(seed)
    ks = iter(jax.random.split(key, 48))

    def nrm(shape, scale):
        return jax.random.normal(next(ks), shape, F32) * scale

    return {
        'x_prompt': nrm((BATCH, SEQ, D_MODEL), 1.0),
        'x_sample': nrm((DEC_BATCH, DEC_SEQ, D_MODEL), 1.0),
        'c': nrm((DEC_BATCH, D_MODEL), 1.0),
        'cache_a_k': nrm((DEC_BATCH, N_EVEN, A_HEADS, PAST_LEN, 2 * A_DK), 1.0),
        'cache_a_v': nrm((DEC_BATCH, N_EVEN, A_HEADS, PAST_LEN, A_DV), 1.0),
        'cache_d_k': nrm((DEC_BATCH, N_ODD, D_KV_HEADS, PAST_LEN, D_HEAD_DIM), 1.0),
        'cache_d_v': nrm((DEC_BATCH, N_ODD, D_KV_HEADS, PAST_LEN, D_HEAD_DIM), 1.0),
        'state_c_ssm': nrm((DEC_BATCH, N_ODD, 2, 2, C_GROUPS, C_STATE), 0.5),
        'c_ctx': nrm((D_MODEL,), 1.0),
        'ada_w': nrm((DEPTH, D_MODEL, N_MOD * D_MODEL), 0.5 * D_MODEL ** -0.5),
        'ada_b': nrm((DEPTH, N_MOD * D_MODEL), 0.02),
        'norm_g': 1.0 + nrm((DEPTH, 2, D_MODEL), 0.02),
        'mlp_w1': nrm((DEPTH, D_MODEL, D_FF), D_MODEL ** -0.5),
        'mlp_w2': nrm((DEPTH, D_FF, D_MODEL), D_FF ** -0.5),
        'even_w_in': nrm((N_EVEN, D_MODEL, EVEN_IN), D_MODEL ** -0.5),
        'even_w_out': nrm((N_EVEN, A_WIDTH + B_WIDTH, D_MODEL), (A_WIDTH + B_WIDTH) ** -0.5),
        'diff_lambda': nrm((N_EVEN, 4, A_DK), 0.1),
        'diff_subln': 1.0 + nrm((N_EVEN, A_DV), 0.02),
        'conv_w': nrm((N_EVEN, B_KERNEL, B_WIDTH), B_KERNEL ** -0.5),
        'conv_b': nrm((N_EVEN, B_WIDTH), 0.02),
        'conv_ln': jnp.stack([1.0 + nrm((N_EVEN, B_WIDTH), 0.02), nrm((N_EVEN, B_WIDTH), 0.02)], axis=1),
        'odd_w_in': nrm((N_ODD, D_MODEL, ODD_IN), D_MODEL ** -0.5),
        'odd_w_out': nrm((N_ODD, C_WIDTH + D_WIDTH, D_MODEL), (C_WIDTH + D_WIDTH) ** -0.5),
        'ssm_a_re': -0.5 + nrm((N_ODD, 2, C_GROUPS, C_STATE), 0.01),
        'ssm_a_im': jnp.pi * jnp.arange(C_STATE, dtype=F32) + nrm((N_ODD, 2, C_GROUPS, C_STATE), 0.01),
        'ssm_log_dt': jax.random.uniform(next(ks), (N_ODD, 2, C_GROUPS), F32,
                                         minval=math.log(1e-3), maxval=math.log(1e-1)),
        'ssm_b': nrm((N_ODD, 2, 2, C_GROUPS, C_STATE, C_GROUP), (2 * C_GROUP) ** -0.5),
        'ssm_c': nrm((N_ODD, 2, 2, C_GROUPS, C_GROUP, C_STATE), C_STATE ** -0.5),
        'ssm_d': nrm((N_ODD, C_WIDTH), 0.5),
        'ssm_glu_w': nrm((N_ODD, C_WIDTH, C_WIDTH), C_WIDTH ** -0.5),
        'ssm_glu_b': nrm((N_ODD, C_WIDTH), 0.02),
        'qk_norm': 1.0 + nrm((N_ODD, 2, D_HEAD_DIM), 0.02),
        'final_norm': 1.0 + nrm((D_MODEL,), 0.02),
    }


def reference(x_prompt, x_sample, c, cache_a_k, cache_a_v, cache_d_k, cache_d_v, state_c_ssm,
              c_ctx, ada_w, ada_b, norm_g, mlp_w1, mlp_w2, even_w_in, even_w_out, diff_lambda,
              diff_subln, conv_w, conv_b, conv_ln, odd_w_in, odd_w_out, ssm_a_re, ssm_a_im,
              ssm_log_dt, ssm_b, ssm_c, ssm_d, ssm_glu_w, ssm_glu_b, qk_norm, final_norm):
    n_lat = x_sample.shape[1]
    rope_a = axial_rope(n_lat, A_DK)
    rope_d = axial_rope(n_lat, D_HEAD_DIM)
    bp = x_prompt.shape[0]
    xp, xs = x_prompt, x_sample
    a_k, a_v, d_k, d_v, c_st = [], [], [], [], []
    for l in range(DEPTH):
        sp1, cp1, gp1, sp2, cp2, gp2 = ada_modulation(c_ctx[None, :], ada_w[l], ada_b[l])
        ss1, cs1, gs1, ss2, cs2, gs2 = ada_modulation(c, ada_w[l], ada_b[l])
        hp = modulate(xp, norm_g[l, 0], sp1, cp1)
        hs = modulate(xs, norm_g[l, 0], ss1, cs1)
        if l % 2 == 0:
            e = l // 2
            lam_init = 0.8 - 0.6 * math.exp(-0.3 * l)
            mp, kp, vp = even_mixer(hp, lam_init, even_w_in[e], even_w_out[e], diff_lambda[e],
                                    diff_subln[e], conv_w[e], conv_b[e], conv_ln[e], None, None, None)
            ms, _, _ = even_mixer(hs, lam_init, even_w_in[e], even_w_out[e], diff_lambda[e],
                                  diff_subln[e], conv_w[e], conv_b[e], conv_ln[e],
                                  cache_a_k[:, e], cache_a_v[:, e], rope_a)
            a_k.append(kp)
            a_v.append(vp)
        else:
            o = l // 2
            h0p = jnp.zeros((bp, 2, 2, C_GROUPS, C_STATE), F32)
            mp, kp, vp, stp = odd_mixer(hp, odd_w_in[o], odd_w_out[o], ssm_a_re[o], ssm_a_im[o],
                                        ssm_log_dt[o], ssm_b[o], ssm_c[o], ssm_d[o], ssm_glu_w[o],
                                        ssm_glu_b[o], qk_norm[o], None, None, h0p, None)
            ms, _, _, _ = odd_mixer(hs, odd_w_in[o], odd_w_out[o], ssm_a_re[o], ssm_a_im[o],
                                    ssm_log_dt[o], ssm_b[o], ssm_c[o], ssm_d[o], ssm_glu_w[o],
                                    ssm_glu_b[o], qk_norm[o], cache_d_k[:, o], cache_d_v[:, o],
                                    state_c_ssm[:, o], rope_d)
            d_k.append(kp)
            d_v.append(vp)
            c_st.append(stp)
        xp = xp + gp1 * mp
        xs = xs + gs1 * ms
        xp = xp + gp2 * sq_relu_mlp(modulate(xp, norm_g[l, 1], sp2, cp2), mlp_w1[l], mlp_w2[l])
        xs = xs + gs2 * sq_relu_mlp(modulate(xs, norm_g[l, 1], ss2, cs2), mlp_w1[l], mlp_w2[l])
    y_prompt = rms_norm(xp, final_norm)
    y_sample = rms_norm(xs, final_norm)
    new_a_k = jnp.stack(a_k, axis=1)
    new_a_v = jnp.stack(a_v, axis=1)
    new_d_k = jnp.stack(d_k, axis=1)
    new_d_v = jnp.stack(d_v, axis=1)
    new_c_ssm = jnp.stack(c_st, axis=1)
    return (y_prompt, y_sample, new_a_k, new_a_v, new_d_k, new_d_v, new_c_ssm)
```

```python
import functools
import math

import jax
import jax.numpy as jnp
from jax import lax
from jax.experimental import pallas as pl
from jax.experimental.pallas import tpu as pltpu

F32 = jnp.float32
BF16 = jnp.bfloat16

EPS = 1e-6
ROPE_BASE = 10000.0
GRID_W = 64
N_MOD = 6
A_HEADS = 8
A_DK = 64
A_DV = 2 * A_DK
B_KERNEL = 31
B_PAD = (B_KERNEL - 1) // 2
C_GROUP = 16
C_STATE = 64
D_HEADS = 8
D_KV_HEADS = 2
D_REP = D_HEADS // D_KV_HEADS
D_HEAD_DIM = 128

LANES = 128
SUBLANES = 8
VMEM_LIMIT = 56 * 1024 * 1024

S5_GROUP_BLOCKS = 4


def _params(*sem):
    return pltpu.CompilerParams(dimension_semantics=sem, vmem_limit_bytes=VMEM_LIMIT)


def _mod_row(row0, n_prompt_rows, dec_seq):
    return jnp.maximum(row0 + (dec_seq - n_prompt_rows), 0) // dec_seq


def _ada_kernel(c_ref, w_ref, b_ref, o_ref):
    c = c_ref[...]
    s = (c * jax.nn.sigmoid(c)).astype(BF16)
    o_ref[0] = jnp.dot(s, w_ref[0].astype(BF16), preferred_element_type=F32) + b_ref[0]


def _ada(cond8, ada_w, ada_b):
    depth, d, n = ada_w.shape
    tn = 1024
    return pl.pallas_call(
        _ada_kernel,
        grid=(depth, n // tn),
        in_specs=[pl.BlockSpec((SUBLANES, d), lambda l, j: (0, 0)),
                  pl.BlockSpec((1, d, tn), lambda l, j: (l, 0, j)),
                  pl.BlockSpec((1, 1, tn), lambda l, j: (l, 0, j))],
        out_specs=pl.BlockSpec((1, SUBLANES, tn), lambda l, j: (l, 0, j)),
        out_shape=jax.ShapeDtypeStruct((depth, SUBLANES, n), F32),
        compiler_params=_params("parallel", "parallel"),
        name="ada_modulation",
    )(cond8, ada_w, ada_b.reshape(depth, 1, n))


def _rms(x):
    return x * lax.rsqrt(jnp.mean(x * x, axis=-1, keepdims=True) + EPS)


def _norm_mm_kernel(x_ref, g_ref, sh_ref, sc_ref, w_ref, o_ref, h_ref, *, relu2):
    @pl.when(pl.program_id(1) == 0)
    def _():
        y = _rms(x_ref[...]) * g_ref[...]
        h_ref[...] = (y * (1.0 + sc_ref[0, 0]) + sh_ref[0, 0]).astype(BF16)

    acc = jnp.dot(h_ref[...], w_ref[...], preferred_element_type=F32)
    if relu2:
        acc = jnp.maximum(acc, 0.0)
        acc = acc * acc
    o_ref[...] = acc.astype(o_ref.dtype)


def _norm_mm(x, g, mods, layer, chunk, w, *, relu2, out_dtype, tm, tn, n_prompt_rows, dec_seq):
    m, d = x.shape
    n = w.shape[1]
    row = lambda i: _mod_row(i * tm, n_prompt_rows, dec_seq)
    return pl.pallas_call(
        functools.partial(_norm_mm_kernel, relu2=relu2),
        grid=(m // tm, n // tn),
        in_specs=[pl.BlockSpec((tm, d), lambda i, j: (i, 0)),
                  pl.BlockSpec((1, d), lambda i, j: (0, 0)),
                  pl.BlockSpec((1, 1, 1, d), lambda i, j: (layer, row(i), 0, chunk)),
                  pl.BlockSpec((1, 1, 1, d), lambda i, j: (layer, row(i), 0, chunk + 1)),
                  pl.BlockSpec((d, tn), lambda i, j: (0, j))],
        out_specs=pl.BlockSpec((tm, tn), lambda i, j: (i, j)),
        out_shape=jax.ShapeDtypeStruct((m, n), out_dtype),
        scratch_shapes=[pltpu.VMEM((tm, d), BF16)],
        compiler_params=_params("parallel", "arbitrary"),
        name="norm_mod_matmul",
    )(x, g.reshape(1, d), mods, mods, w)


def _mm_res_kernel(*refs, n_parts, nk):
    a_refs = refs[:n_parts]
    w_refs = refs[n_parts:2 * n_parts]
    res_ref, gate_ref, o_ref = refs[2 * n_parts:2 * n_parts + 3]
    part = jnp.dot(a_refs[0][...], w_refs[0][...], preferred_element_type=F32)
    for a_ref, w_ref in zip(a_refs[1:], w_refs[1:]):
        part = part + jnp.dot(a_ref[...], w_ref[...], preferred_element_type=F32)
    if nk == 1:
        o_ref[...] = res_ref[...] + gate_ref[0, 0] * part
        return
    acc_ref = refs[-1]
    k = pl.program_id(2)

    @pl.when(k == 0)
    def _():
        acc_ref[...] = part

    @pl.when(k > 0)
    def _():
        acc_ref[...] += part

    @pl.when(k == nk - 1)
    def _():
        o_ref[...] = res_ref[...] + gate_ref[0, 0] * acc_ref[...]


def _mm_res(a_parts, w_parts, res, mods, layer, chunk, *, tm, tn, tk, n_prompt_rows, dec_seq):
    m, n = res.shape
    n_parts = len(a_parts)
    kdim = a_parts[0].shape[1]
    nk = kdim // tk
    gate_blk = chunk * (n // tn)
    row = lambda i: _mod_row(i * tm, n_prompt_rows, dec_seq)
    in_specs = ([pl.BlockSpec((tm, tk), lambda i, j, k: (i, k)) for _ in a_parts]
                + [pl.BlockSpec((tk, tn), lambda i, j, k: (k, j)) for _ in w_parts]
                + [pl.BlockSpec((tm, tn), lambda i, j, k: (i, j)),
                   pl.BlockSpec((1, 1, 1, tn), lambda i, j, k: (layer, row(i), 0, gate_blk + j))])
    scratch = [pltpu.VMEM((tm, tn), F32)] if nk > 1 else []
    return pl.pallas_call(
        functools.partial(_mm_res_kernel, n_parts=n_parts, nk=nk),
        grid=(m // tm, n // tn, nk),
        in_specs=in_specs,
        out_specs=pl.BlockSpec((tm, tn), lambda i, j, k: (i, j)),
        out_shape=jax.ShapeDtypeStruct((m, n), F32),
        scratch_shapes=scratch,
        compiler_params=_params("parallel", "parallel", "arbitrary"),
        name="matmul_gated_residual",
    )(*a_parts, *w_parts, res, mods)


def _softmax(s):
    e = jnp.exp(s - jnp.max(s, axis=-1, keepdims=True))
    return e / jnp.sum(e, axis=-1, keepdims=True)


_QK_DIMS = (((1,), (1,)), ((), ()))


def _rope(x, cos, sin_up, sin_dn, quarter):
    width = x.shape[-1]
    return (x * cos + pltpu.roll(x, width - quarter, axis=1) * sin_up
            + pltpu.roll(x, quarter, axis=1) * sin_dn)


def _rope_tables(n_tokens, dim, reps):
    rows = n_tokens // GRID_W
    row = jnp.repeat(jnp.arange(rows, dtype=F32), GRID_W)
    col = jnp.tile(jnp.arange(GRID_W, dtype=F32), rows)
    quarter = dim // 4
    inv_freq = ROPE_BASE ** (-jnp.arange(quarter, dtype=F32) / quarter)
    ang_r = row[:, None] * inv_freq[None, :]
    ang_c = col[:, None] * inv_freq[None, :]
    ang = jnp.concatenate([ang_r, ang_r, ang_c, ang_c], axis=-1)
    cos, sin = jnp.cos(ang), jnp.sin(ang)
    even_chunk = ((jnp.arange(dim) // quarter) % 2 == 0)[None, :]
    sin_up = jnp.where(even_chunk, -sin, 0.0)
    sin_dn = jnp.where(even_chunk, 0.0, sin)
    tile = lambda t: jnp.tile(t, (1, reps))
    return tile(cos), tile(sin_up), tile(sin_dn)


def _diff_lambda(lam_ref, lam_init):
    lp = lam_ref[...]
    a = jnp.sum(lp[0:1] * lp[1:2], axis=-1, keepdims=True)
    b = jnp.sum(lp[2:3] * lp[3:4], axis=-1, keepdims=True)
    return jnp.exp(a) - jnp.exp(b) + lam_init


def _diff_attend(q, kb, vb, lam, subln, lam_init):
    lane = lax.broadcasted_iota(jnp.int32, q.shape, 1)
    q1 = jnp.where(lane < A_DK, q, 0.0).astype(BF16)
    q2 = jnp.where(lane >= A_DK, q, 0.0).astype(BF16)
    scale = A_DK ** -0.5
    s1 = lax.dot_general(q1, kb, _QK_DIMS, preferred_element_type=F32) * scale
    s2 = lax.dot_general(q2, kb, _QK_DIMS, preferred_element_type=F32) * scale
    p = _softmax(s1) - lam * _softmax(s2)
    o = jnp.dot(p.astype(BF16), vb, preferred_element_type=F32)
    return _rms(o) * subln * (1.0 - lam_init)


def _dattn_prompt_kernel(q_ref, k_ref, v_ref, lam_ref, sub_ref, o_ref, ko_ref, vo_ref, *, lam_init):
    k = k_ref[...]
    v = v_ref[...]
    ko_ref[0, 0] = k
    vo_ref[0, 0] = v
    lam = _diff_lambda(lam_ref, lam_init)
    o = _diff_attend(q_ref[...], k.astype(BF16), v.astype(BF16), lam, sub_ref[...], lam_init)
    o_ref[...] = o.astype(o_ref.dtype)


def _dattn_prompt(proj, lam_p, subln, lam_init, *, batch, seq, n_rows):
    hd = 2 * A_DK
    kv_shape = jax.ShapeDtypeStruct((batch, A_HEADS, seq, hd), F32)
    return pl.pallas_call(
        functools.partial(_dattn_prompt_kernel, lam_init=lam_init),
        grid=(batch, A_HEADS),
        in_specs=[pl.BlockSpec((seq, hd), lambda b, h: (b, h)),
                  pl.BlockSpec((seq, hd), lambda b, h: (b, A_HEADS + h)),
                  pl.BlockSpec((seq, hd), lambda b, h: (b, 2 * A_HEADS + h)),
                  pl.BlockSpec(lam_p.shape, lambda b, h: (0, 0)),
                  pl.BlockSpec((1, hd), lambda b, h: (0, 0))],
        out_specs=[pl.BlockSpec((seq, hd), lambda b, h: (b, h)),
                   pl.BlockSpec((1, 1, seq, hd), lambda b, h: (b, h, 0, 0)),
                   pl.BlockSpec((1, 1, seq, hd), lambda b, h: (b, h, 0, 0))],
        out_shape=[jax.ShapeDtypeStruct((n_rows, A_HEADS * A_DV), BF16), kv_shape, kv_shape],
        compiler_params=_params("parallel", "parallel"),
        name="diff_attention_prompt",
    )(proj, proj, proj, lam_p, subln.reshape(1, hd))


def _dattn_sample_kernel(q_ref, k_ref, v_ref, ck_ref, cv_ref, cos_ref, su_ref, sd_ref, lam_ref,
                         sub_ref, att_in_ref, o_ref, kf_ref, vf_ref, *, lam_init, tq):
    del att_in_ref
    t = k_ref.shape[0]
    quarter = A_DK // 4
    kf_ref[0:t] = _rope(k_ref[...], cos_ref[...], su_ref[...], sd_ref[...], quarter).astype(BF16)
    kf_ref[t:] = ck_ref[0, 0, 0].astype(BF16)
    vf_ref[0:t] = v_ref[...].astype(BF16)
    vf_ref[t:] = cv_ref[0, 0, 0].astype(BF16)
    lam = _diff_lambda(lam_ref, lam_init)
    for i in range(t // tq):
        rows = slice(i * tq, (i + 1) * tq)
        q = _rope(q_ref[rows, :], cos_ref[rows, :], su_ref[rows, :], sd_ref[rows, :], quarter)
        o = _diff_attend(q, kf_ref[...], vf_ref[...], lam, sub_ref[...], lam_init)
        o_ref[rows, :] = o.astype(o_ref.dtype)


def _dattn_sample(proj, cache_k, cache_v, e, rope, lam_p, subln, lam_init, att, *,
                  dec_batch, dec_seq, n_prompt_rows):
    hd = 2 * A_DK
    past = cache_k.shape[3]
    rb = n_prompt_rows // dec_seq
    cache_spec = pl.BlockSpec((1, 1, 1, past, hd), lambda b, h: (b, e, h, 0, 0))
    tab_spec = pl.BlockSpec((dec_seq, hd), lambda b, h: (0, 0))
    return pl.pallas_call(
        functools.partial(_dattn_sample_kernel, lam_init=lam_init, tq=256),
        grid=(dec_batch, A_HEADS),
        in_specs=[pl.BlockSpec((dec_seq, hd), lambda b, h: (rb + b, h)),
                  pl.BlockSpec((dec_seq, hd), lambda b, h: (rb + b, A_HEADS + h)),
                  pl.BlockSpec((dec_seq, hd), lambda b, h: (rb + b, 2 * A_HEADS + h)),
                  cache_spec, cache_spec, tab_spec, tab_spec, tab_spec,
                  pl.BlockSpec(lam_p.shape, lambda b, h: (0, 0)),
                  pl.BlockSpec((1, hd), lambda b, h: (0, 0)),
                  pl.BlockSpec(memory_space=pl.ANY)],
        out_specs=pl.BlockSpec((dec_seq, hd), lambda b, h: (rb + b, h)),
        out_shape=jax.ShapeDtypeStruct(att.shape, att.dtype),
        scratch_shapes=[pltpu.VMEM((dec_seq + past, hd), BF16),
                        pltpu.VMEM((dec_seq + past, hd), BF16)],
        input_output_aliases={10: 0},
        compiler_params=_params("parallel", "parallel"),
        name="diff_attention_sample",
    )(proj, proj, proj, cache_k, cache_v, *rope, lam_p, subln.reshape(1, hd), att)


def _gqa_attend(q, kb, vb):
    s = lax.dot_general(q.astype(BF16), kb, _QK_DIMS, preferred_element_type=F32) * (D_HEAD_DIM ** -0.5)
    return jnp.dot(_softmax(s).astype(BF16), vb, preferred_element_type=F32)


def _gqa_prompt_kernel(q_ref, k_ref, v_ref, g_ref, o_ref, ko_ref, vo_ref):
    gq = g_ref[0:1]
    k = _rms(k_ref[...]) * g_ref[1:2]
    v = v_ref[...]
    ko_ref[0, 0] = k
    vo_ref[0, 0] = v
    kb = k.astype(BF16)
    vb = v.astype(BF16)
    for r in range(D_REP):
        cols = slice(r * D_HEAD_DIM, (r + 1) * D_HEAD_DIM)
        q = _rms(q_ref[:, cols]) * gq
        o_ref[:, cols] = _gqa_attend(q, kb, vb).astype(o_ref.dtype)


def _gqa_prompt(proj, qk_g, *, batch, seq, n_rows, c_width):
    hd = D_HEAD_DIM
    qw = D_REP * hd
    q0 = c_width // qw
    k0 = (c_width + D_HEADS * hd) // hd
    v0 = k0 + D_KV_HEADS
    kv_shape = jax.ShapeDtypeStruct((batch, D_KV_HEADS, seq, hd), F32)
    return pl.pallas_call(
        _gqa_prompt_kernel,
        grid=(batch, D_KV_HEADS),
        in_specs=[pl.BlockSpec((seq, qw), lambda b, g: (b, q0 + g)),
                  pl.BlockSpec((seq, hd), lambda b, g: (b, k0 + g)),
                  pl.BlockSpec((seq, hd), lambda b, g: (b, v0 + g)),
                  pl.BlockSpec(qk_g.shape, lambda b, g: (0, 0))],
        out_specs=[pl.BlockSpec((seq, qw), lambda b, g: (b, g)),
                   pl.BlockSpec((1, 1, seq, hd), lambda b, g: (b, g, 0, 0)),
                   pl.BlockSpec((1, 1, seq, hd), lambda b, g: (b, g, 0, 0))],
        out_shape=[jax.ShapeDtypeStruct((n_rows, D_HEADS * hd), BF16), kv_shape, kv_shape],
        compiler_params=_params("parallel", "parallel"),
        name="gqa_prompt",
    )(proj, proj, proj, qk_g)


def _gqa_sample_kernel(q_ref, k_ref, v_ref, ck_ref, cv_ref, cos_ref, su_ref, sd_ref, g_ref,
                       att_in_ref, o_ref, kf_ref, vf_ref, *, tq):
    del att_in_ref
    t = k_ref.shape[0]
    quarter = D_HEAD_DIM // 4
    gq = g_ref[0:1]
    k = _rms(k_ref[...]) * g_ref[1:2]
    kf_ref[0:t] = _rope(k, cos_ref[...], su_ref[...], sd_ref[...], quarter).astype(BF16)
    kf_ref[t:] = ck_ref[0, 0, 0].astype(BF16)
    vf_ref[0:t] = v_ref[...].astype(BF16)
    vf_ref[t:] = cv_ref[0, 0, 0].astype(BF16)
    for r in range(D_REP):
        cols = slice(r * D_HEAD_DIM, (r + 1) * D_HEAD_DIM)
        for i in range(t // tq):
            rows = slice(i * tq, (i + 1) * tq)
            q = _rms(q_ref[rows, cols]) * gq
            q = _rope(q, cos_ref[rows, :], su_ref[rows, :], sd_ref[rows, :], quarter)
            o_ref[rows, cols] = _gqa_attend(q, kf_ref[...], vf_ref[...]).astype(o_ref.dtype)


def _gqa_sample(proj, cache_k, cache_v, o, rope, qk_g, att, *, dec_batch, dec_seq, n_prompt_rows,
                c_width):
    hd = D_HEAD_DIM
    qw = D_REP * hd
    q0 = c_width // qw
    k0 = (c_width + D_HEADS * hd) // hd
    v0 = k0 + D_KV_HEADS
    past = cache_k.shape[3]
    rb = n_prompt_rows // dec_seq
    cache_spec = pl.BlockSpec((1, 1, 1, past, hd), lambda b, g: (b, o, g, 0, 0))
    tab_spec = pl.BlockSpec((dec_seq, hd), lambda b, g: (0, 0))
    return pl.pallas_call(
        functools.partial(_gqa_sample_kernel, tq=256),
        grid=(dec_batch, D_KV_HEADS),
        in_specs=[pl.BlockSpec((dec_seq, qw), lambda b, g: (rb + b, q0 + g)),
                  pl.BlockSpec((dec_seq, hd), lambda b, g: (rb + b, k0 + g)),
                  pl.BlockSpec((dec_seq, hd), lambda b, g: (rb + b, v0 + g)),
                  cache_spec, cache_spec, tab_spec, tab_spec, tab_spec,
                  pl.BlockSpec(qk_g.shape, lambda b, g: (0, 0)),
                  pl.BlockSpec(memory_space=pl.ANY)],
        out_specs=pl.BlockSpec((dec_seq, qw), lambda b, g: (rb + b, g)),
        out_shape=jax.ShapeDtypeStruct(att.shape, att.dtype),
        scratch_shapes=[pltpu.VMEM((dec_seq + past, hd), BF16),
                        pltpu.VMEM((dec_seq + past, hd), BF16)],
        input_output_aliases={9: 0},
        compiler_params=_params("parallel", "parallel"),
        name="gqa_sample",
    )(proj, proj, proj, cache_k, cache_v, *rope, qk_g, att)


CONV_ROWS = 256
CONV_HALO = 16


def _conv_kernel(a_ref, g_ref, ap_ref, gp_ref, an_ref, gn_ref, w_ref, b_ref, lng_ref, lnb_ref,
                 o_ref, pad_ref, cv_ref, *, n_prompt_blocks, blocks_per_seq):
    rows, width = a_ref.shape
    n_chunks = width // LANES
    i = pl.program_id(0)
    j = jnp.maximum(i - n_prompt_blocks, 0) % blocks_per_seq
    latent = i >= n_prompt_blocks
    has_prev = jnp.logical_and(latent, j > 0).astype(F32)
    has_next = jnp.logical_and(latent, j < blocks_per_seq - 1).astype(F32)

    glu = lambda a, g: a * jax.nn.sigmoid(g)
    h_prev = glu(ap_ref[...], gp_ref[...]) * has_prev
    h_cur = glu(a_ref[...], g_ref[...])
    h_next = glu(an_ref[...], gn_ref[...]) * has_next
    for c in range(n_chunks):
        lanes = slice(c * LANES, (c + 1) * LANES)
        pad_ref[c, 0:CONV_HALO, :] = h_prev[:, lanes]
        pad_ref[c, CONV_HALO:CONV_HALO + rows, :] = h_cur[:, lanes]
        pad_ref[c, CONV_HALO + rows:, :] = h_next[:, lanes]

    def chunk_body(c, carry):
        for r in range(rows // SUBLANES):
            base = CONV_HALO - B_PAD + r * SUBLANES
            acc = w_ref[c, 0] * pad_ref[c, pl.ds(base, SUBLANES), :]
            for tap in range(1, B_KERNEL):
                acc = acc + w_ref[c, tap] * pad_ref[c, pl.ds(base + tap, SUBLANES), :]
            cv_ref[c, pl.ds(r * SUBLANES, SUBLANES), :] = acc
        return carry

    lax.fori_loop(0, n_chunks, chunk_body, 0)

    x = jnp.concatenate([cv_ref[c] for c in range(n_chunks)], axis=1) + b_ref[...]
    mu = jnp.mean(x, axis=-1, keepdims=True)
    xc = x - mu
    var = jnp.mean(xc * xc, axis=-1, keepdims=True)
    y = xc * lax.rsqrt(var + EPS) * lng_ref[...] + lnb_ref[...]
    o_ref[...] = (y * jax.nn.sigmoid(y)).astype(o_ref.dtype)


def _conv(proj, conv_w, conv_b, conv_ln, *, col0, n_prompt_rows, prompt_seq, dec_seq):
    m = proj.shape[0]
    taps, width = conv_w.shape
    assert prompt_seq == CONV_ROWS and dec_seq % CONV_ROWS == 0
    n_chunks = width // LANES
    a_blk = col0 // width
    halo_per_blk = CONV_ROWS // CONV_HALO
    last_halo = m // CONV_HALO - 1
    prev_map = lambda cb: (lambda i: (jnp.maximum(i * halo_per_blk - 1, 0), cb))
    next_map = lambda cb: (lambda i: (jnp.minimum((i + 1) * halo_per_blk, last_halo), cb))
    w8 = jnp.broadcast_to(conv_w.reshape(taps, 1, n_chunks, LANES), (taps, SUBLANES, n_chunks, LANES))
    w8 = w8.transpose(2, 0, 1, 3)
    row_spec = lambda: pl.BlockSpec((1, width), lambda i: (0, 0))
    return pl.pallas_call(
        functools.partial(_conv_kernel, n_prompt_blocks=n_prompt_rows // CONV_ROWS,
                          blocks_per_seq=dec_seq // CONV_ROWS),
        grid=(m // CONV_ROWS,),
        in_specs=[pl.BlockSpec((CONV_ROWS, width), lambda i: (i, a_blk)),
                  pl.BlockSpec((CONV_ROWS, width), lambda i: (i, a_blk + 1)),
                  pl.BlockSpec((CONV_HALO, width), prev_map(a_blk)),
                  pl.BlockSpec((CONV_HALO, width), prev_map(a_blk + 1)),
                  pl.BlockSpec((CONV_HALO, width), next_map(a_blk)),
                  pl.BlockSpec((CONV_HALO, width), next_map(a_blk + 1)),
                  pl.BlockSpec(w8.shape, lambda i: (0, 0, 0, 0)),
                  row_spec(), row_spec(), row_spec()],
        out_specs=pl.BlockSpec((CONV_ROWS, width), lambda i: (i, 0)),
        out_shape=jax.ShapeDtypeStruct((m, width), BF16),
        scratch_shapes=[pltpu.VMEM((n_chunks, CONV_ROWS + 2 * CONV_HALO, LANES), F32),
                        pltpu.VMEM((n_chunks, CONV_ROWS, LANES), F32)],
        compiler_params=_params("parallel"),
        name="conformer_conv",
    )(proj, proj, proj, proj, proj, proj, w8, conv_b.reshape(1, width),
      conv_ln[0].reshape(1, width), conv_ln[1].reshape(1, width))


def _s5_disc_kernel(ar_ref, ai_ref, ldt_ref, br_ref, bi_ref, abr_ref, abi_ref, bbr_ref, bbi_ref):
    ar = ar_ref[0]
    ai = ai_ref[0]
    dt = jnp.exp(ldt_ref[0])
    mag = jnp.exp(ar * dt)
    abr = mag * jnp.cos(ai * dt)
    abi = mag * jnp.sin(ai * dt)
    den = ar * ar + ai * ai
    nr = abr - 1.0
    ni = abi
    kr = (nr * ar + ni * ai) / den
    ki = (ni * ar - nr * ai) / den
    br = br_ref[0]
    bi = bi_ref[0]
    abr_ref[0] = abr
    abi_ref[0] = abi
    bbr_ref[0] = kr * br - ki * bi
    bbi_ref[0] = kr * bi + ki * br


def _s5_discretize(a_re, a_im, log_dt, b):
    nd, g, p = a_re.shape
    h = b.shape[-1]
    rep = lambda a: jnp.broadcast_to(a[..., None], (nd, g, p, h)).reshape(nd, g, p * h)
    ldt = jnp.broadcast_to(log_dt[:, :, None], (nd, g, p * h))
    spec = pl.BlockSpec((1, g, p * h), lambda d: (d, 0, 0))
    shape = jax.ShapeDtypeStruct((nd, g, p * h), F32)
    abr, abi, bbr, bbi = pl.pallas_call(
        _s5_disc_kernel,
        grid=(nd,),
        in_specs=[spec] * 5,
        out_specs=[spec] * 4,
        out_shape=[shape] * 4,
        compiler_params=_params("parallel"),
        name="s5_discretize",
    )(rep(a_re), rep(a_im), ldt, b[:, 0].reshape(nd, g, p * h), b[:, 1].reshape(nd, g, p * h))
    pick = lambda a: a.reshape(nd, g, p, h)[..., 0]
    return pick(abr), pick(abi), bbr.reshape(nd, g, p, h), bbi.reshape(nd, g, p, h)


def _s5_scan_kernel(u_ref, wb_ref, wc_ref, a_ref, h0_ref, y_ref, hf_ref, bu_ref, hs_ref, *,
                    rows, steps, lane_chunk, unroll):
    d = pl.program_id(0)
    t = pl.program_id(2)
    half = bu_ref.shape[1] // 2

    @pl.when(t == 0)
    def _():
        hs_ref[...] = h0_ref[0, 0]

    bu_ref[...] = jnp.dot(u_ref[...], wb_ref[0, 0], preferred_element_type=F32)
    reverse = d == 1
    for c in range(half // lane_chunk):
        lr = slice(c * lane_chunk, (c + 1) * lane_chunk)
        li = slice(half + c * lane_chunk, half + (c + 1) * lane_chunk)
        ar = jnp.broadcast_to(a_ref[0, 0, :, lr], (rows, lane_chunk))
        ai = jnp.broadcast_to(a_ref[0, 0, :, li], (rows, lane_chunk))

        def step(i, carry):
            hr, hi = carry
            s = jnp.where(reverse, steps - 1 - i, i)
            row = pl.multiple_of(s * rows, rows)
            nr = ar * hr - ai * hi + bu_ref[pl.ds(row, rows), lr]
            ni = ar * hi + ai * hr + bu_ref[pl.ds(row, rows), li]
            bu_ref[pl.ds(row, rows), lr] = nr
            bu_ref[pl.ds(row, rows), li] = ni
            return nr, ni

        hr, hi = lax.fori_loop(0, steps, step, (hs_ref[:, lr], hs_ref[:, li]), unroll=unroll)
        hs_ref[:, lr] = hr
        hs_ref[:, li] = hi

    y_ref[0] = jnp.dot(bu_ref[...].astype(BF16), wc_ref[0, 0], preferred_element_type=F32)
    hf_ref[0, 0] = hs_ref[...]


def _s5_scan(u_tm, wb, wc, abar, h0, *, rows, steps, lane_chunk, unroll):
    n, width = u_tm.shape
    nb = S5_GROUP_BLOCKS
    ub = width // nb
    sb = wb.shape[-1]
    m = rows * steps
    nt = n // m
    tblk = lambda d, t: jnp.where(d == 0, t, nt - 1 - t)
    return pl.pallas_call(
        functools.partial(_s5_scan_kernel, rows=rows, steps=steps, lane_chunk=lane_chunk,
                          unroll=unroll),
        grid=(2, nb, nt),
        in_specs=[pl.BlockSpec((m, ub), lambda d, g, t: (tblk(d, t), g)),
                  pl.BlockSpec((1, 1, ub, sb), lambda d, g, t: (d, g, 0, 0)),
                  pl.BlockSpec((1, 1, sb, ub), lambda d, g, t: (d, g, 0, 0)),
                  pl.BlockSpec((1, 1, 1, sb), lambda d, g, t: (d, g, 0, 0)),
                  pl.BlockSpec((1, 1, rows, sb), lambda d, g, t: (d, g, 0, 0))],
        out_specs=[pl.BlockSpec((1, m, ub), lambda d, g, t: (d, tblk(d, t), g)),
                   pl.BlockSpec((1, 1, rows, sb), lambda d, g, t: (d, g, 0, 0))],
        out_shape=[jax.ShapeDtypeStruct((2, n, width), F32),
                   jax.ShapeDtypeStruct((2, nb, rows, sb), F32)],
        scratch_shapes=[pltpu.VMEM((m, sb), F32), pltpu.VMEM((rows, sb), F32)],
        compiler_params=_params("parallel", "parallel", "arbitrary"),
        name="s5_scan",
    )(u_tm, wb, wc, abar, h0)


def _s5_block_weights(abr, abi, bbr, bbi, c):
    nd, g, p = abr.shape
    h = bbr.shape[-1]
    nb = S5_GROUP_BLOCKS
    gl = g // nb
    eye = jnp.eye(gl, dtype=F32)

    def b_block(bb):
        b5 = bb.reshape(nd, nb, gl, p, h)
        return jnp.einsum('dbgph,gk->dbghkp', b5, eye).reshape(nd, nb, gl * h, gl * p)

    def c_block(cc):
        c5 = cc.reshape(nd, nb, gl, h, p)
        return jnp.einsum('dbgnp,gk->dbgpkn', c5, eye).reshape(nd, nb, gl * p, gl * h)

    wb = jnp.concatenate([b_block(bbr), b_block(bbi)], axis=-1).astype(BF16)
    wc = jnp.concatenate([c_block(c[:, 0]), c_block(-c[:, 1])], axis=-2).astype(BF16)
    lanes = lambda a: a.reshape(nd, nb, 1, gl * p)
    abar = jnp.concatenate([lanes(abr), lanes(abi)], axis=-1)
    return wb, wc, abar


def _s5_glu_kernel(u_ref, yf_ref, yb_ref, d_ref, w_ref, b_ref, o_ref):
    y = d_ref[...] * u_ref[...] + yf_ref[0] + yb_ref[0]
    y = jax.nn.gelu(y)
    z = jnp.dot(y.astype(BF16), w_ref[...], preferred_element_type=F32) + b_ref[...]
    o_ref[...] = (y * jax.nn.sigmoid(z)).astype(o_ref.dtype)


def _s5_glu(proj, y_dirs, ssm_d, glu_w, glu_b, *, tm):
    m = proj.shape[0]
    width = glu_w.shape[0]
    return pl.pallas_call(
        _s5_glu_kernel,
        grid=(m // tm,),
        in_specs=[pl.BlockSpec((tm, width), lambda i: (i, 0)),
                  pl.BlockSpec((1, tm, width), lambda i: (0, i, 0)),
                  pl.BlockSpec((1, tm, width), lambda i: (1, i, 0)),
                  pl.BlockSpec((1, width), lambda i: (0, 0)),
                  pl.BlockSpec((width, width), lambda i: (0, 0)),
                  pl.BlockSpec((1, width), lambda i: (0, 0))],
        out_specs=pl.BlockSpec((tm, width), lambda i: (i, 0)),
        out_shape=jax.ShapeDtypeStruct((m, width), BF16),
        compiler_params=_params("parallel"),
        name="s5_gelu_glu",
    )(proj, y_dirs, y_dirs, ssm_d.reshape(1, width), glu_w, glu_b.reshape(1, width))


def _final_norm_kernel(x_ref, g_ref, o_ref):
    o_ref[...] = _rms(x_ref[...]) * g_ref[...]


def _final_norm(x, g, *, row0, n_rows, tm):
    d = x.shape[1]
    b0 = row0 // tm
    return pl.pallas_call(
        _final_norm_kernel,
        grid=(n_rows // tm,),
        in_specs=[pl.BlockSpec((tm, d), lambda i: (b0 + i, 0)),
                  pl.BlockSpec((1, d), lambda i: (0, 0))],
        out_specs=pl.BlockSpec((tm, d), lambda i: (i, 0)),
        out_shape=jax.ShapeDtypeStruct((n_rows, d), F32),
        compiler_params=_params("parallel"),
        name="final_rms_norm",
    )(x, g.reshape(1, d))


def kernel(x_prompt, x_sample, c, cache_a_k, cache_a_v, cache_d_k, cache_d_v, state_c_ssm, c_ctx,
           ada_w, ada_b, norm_g, mlp_w1, mlp_w2, even_w_in, even_w_out, diff_lambda, diff_subln,
           conv_w, conv_b, conv_ln, odd_w_in, odd_w_out, ssm_a_re, ssm_a_im, ssm_log_dt, ssm_b,
           ssm_c, ssm_d, ssm_glu_w, ssm_glu_b, qk_norm, final_norm):
    batch, seq, d_model = x_prompt.shape
    dec_batch, dec_seq, _ = x_sample.shape
    depth = ada_w.shape[0]
    n_p = batch * seq
    n_s = dec_batch * dec_seq
    n_rows = n_p + n_s
    rows_kw = dict(n_prompt_rows=n_p, dec_seq=dec_seq)
    a_qk = A_HEADS * 2 * A_DK
    a_width = A_HEADS * A_DV
    c_width = ssm_d.shape[1]
    n_groups = c_width // C_GROUP
    assert dec_seq % seq == 0 and 1 + dec_batch <= SUBLANES

    x = jnp.concatenate([x_prompt.reshape(n_p, d_model), x_sample.reshape(n_s, d_model)], axis=0)
    cond8 = jnp.concatenate([c_ctx[None, :], c,
                             jnp.zeros((SUBLANES - 1 - dec_batch, d_model), F32)], axis=0)
    mods = _ada(cond8, ada_w, ada_b).reshape(depth, SUBLANES, 1, N_MOD * d_model)

    rope_a = _rope_tables(dec_seq, A_DK, 2)
    rope_d = _rope_tables(dec_seq, D_HEAD_DIM, 1)

    a_k, a_v, d_k, d_v, c_st = [], [], [], [], []
    for l in range(depth):
        if l % 2 == 0:
            e = l // 2
            lam_init = 0.8 - 0.6 * math.exp(-0.3 * l)
            proj = _norm_mm(x, norm_g[l, 0], mods, l, 0, even_w_in[e].astype(BF16), relu2=False,
                            out_dtype=F32, tm=1024, tn=512, **rows_kw)
            att, kp, vp = _dattn_prompt(proj, diff_lambda[e], diff_subln[e], lam_init,
                                        batch=batch, seq=seq, n_rows=n_rows)
            att = _dattn_sample(proj, cache_a_k, cache_a_v, e, rope_a, diff_lambda[e],
                                diff_subln[e], lam_init, att, dec_batch=dec_batch, dec_seq=dec_seq,
                                n_prompt_rows=n_p)
            conv = _conv(proj, conv_w[e], conv_b[e], conv_ln[e], col0=2 * a_qk + a_width,
                         n_prompt_rows=n_p, prompt_seq=seq, dec_seq=dec_seq)
            w_out = even_w_out[e].astype(BF16)
            x = _mm_res([att, conv], [w_out[:a_width], w_out[a_width:]], x, mods, l, 2,
                        tm=1024, tn=1024, tk=a_width, **rows_kw)
            a_k.append(kp)
            a_v.append(vp)
        else:
            o = l // 2
            proj = _norm_mm(x, norm_g[l, 0], mods, l, 0, odd_w_in[o].astype(BF16), relu2=False,
                            out_dtype=F32, tm=1024, tn=512, **rows_kw)
            abr, abi, bbr, bbi = _s5_discretize(ssm_a_re[o], ssm_a_im[o], ssm_log_dt[o], ssm_b[o])
            wb, wc, abar = _s5_block_weights(abr, abi, bbr, bbi, ssm_c[o])
            nb = S5_GROUP_BLOCKS
            sb = wb.shape[-1]
            u = proj[:, :c_width].astype(BF16)
            u_p = u[:n_p].reshape(batch, seq, c_width).transpose(1, 0, 2).reshape(n_p, c_width)
            y_p, hf_p = _s5_scan(u_p, wb, wc, abar, jnp.zeros((2, nb, batch, sb), F32),
                                 rows=batch, steps=16, lane_chunk=256, unroll=True)
            y_p = y_p.reshape(2, seq, batch, c_width).transpose(0, 2, 1, 3).reshape(2, n_p, c_width)
            u_s = u[n_p:].reshape(dec_batch, dec_seq, c_width).transpose(1, 0, 2)
            u_s = jnp.pad(u_s, ((0, 0), (0, SUBLANES - dec_batch), (0, 0)))
            h0 = state_c_ssm[:, o].reshape(dec_batch, 2, 2, nb, sb // 2)
            h0 = h0.transpose(1, 3, 0, 2, 4).reshape(2, nb, dec_batch, sb)
            h0 = jnp.pad(h0, ((0, 0), (0, 0), (0, SUBLANES - dec_batch), (0, 0)))
            y_s, _ = _s5_scan(u_s.reshape(dec_seq * SUBLANES, c_width), wb, wc, abar, h0,
                              rows=SUBLANES, steps=64, lane_chunk=1024, unroll=8)
            y_s = y_s.reshape(2, dec_seq, SUBLANES, c_width)[:, :, :dec_batch]
            y_s = y_s.transpose(0, 2, 1, 3).reshape(2, n_s, c_width)
            y_dirs = jnp.concatenate([y_p, y_s], axis=1)
            ssm_out = _s5_glu(proj, y_dirs, ssm_d[o], ssm_glu_w[o].astype(BF16), ssm_glu_b[o],
                              tm=1024)
            state = hf_p.reshape(2, nb, batch, 2, n_groups // nb, C_STATE)
            state = state.transpose(2, 0, 3, 1, 4, 5).reshape(batch, 2, 2, n_groups, C_STATE)
            att, kp, vp = _gqa_prompt(proj, qk_norm[o], batch=batch, seq=seq, n_rows=n_rows,
                                      c_width=c_width)
            att = _gqa_sample(proj, cache_d_k, cache_d_v, o, rope_d, qk_norm[o], att,
                              dec_batch=dec_batch, dec_seq=dec_seq, n_prompt_rows=n_p,
                              c_width=c_width)
            w_out = odd_w_out[o].astype(BF16)
            x = _mm_res([ssm_out, att], [w_out[:c_width], w_out[c_width:]], x, mods, l, 2,
                        tm=1024, tn=1024, tk=c_width, **rows_kw)
            d_k.append(kp)
            d_v.append(vp)
            c_st.append(state)
        hidden = _norm_mm(x, norm_g[l, 1], mods, l, 3, mlp_w1[l].astype(BF16), relu2=True,
                          out_dtype=BF16, tm=1024, tn=1024, **rows_kw)
        x = _mm_res([hidden], [mlp_w2[l].astype(BF16)], x, mods, l, 5,
                    tm=1024, tn=1024, tk=2048, **rows_kw)

    y_prompt = _final_norm(x, final_norm, row0=0, n_rows=n_p, tm=512).reshape(batch, seq, d_model)
    y_sample = _final_norm(x, final_norm, row0=n_p, n_rows=n_s, tm=512)
    y_sample = y_sample.reshape(dec_batch, dec_seq, d_model)
    return (y_prompt, y_sample, jnp.stack(a_k, axis=1), jnp.stack(a_v, axis=1),
            jnp.stack(d_k, axis=1), jnp.stack(d_v, axis=1), jnp.stack(c_st, axis=1))
```

```python
import functools
import math

import jax
import jax.numpy as jnp
from jax import lax
from jax.experimental import pallas as pl
from jax.experimental.pallas import tpu as pltpu

F32 = jnp.float32
BF16 = jnp.bfloat16

EPS = 1e-6
ROPE_BASE = 10000.0
GRID_W = 64
N_MOD = 6
A_HEADS = 8
A_DK = 64
A_DV = 2 * A_DK
B_KERNEL = 31
B_PAD = (B_KERNEL - 1) // 2
C_GROUP = 16
C_STATE = 64
D_HEADS = 8
D_KV_HEADS = 2
D_REP = D_HEADS // D_KV_HEADS
D_HEAD_DIM = 128

LANES = 128
SUBLANES = 8
VMEM_LIMIT = 56 * 1024 * 1024

S5_CHUNK = 16
S5_TILE = S5_CHUNK * C_GROUP
S5_GROUPS_PER_STEP = 4
S5_LAG_GROUPS_PER_STEP = 8


def _params(*sem):
    return pltpu.CompilerParams(dimension_semantics=sem, vmem_limit_bytes=VMEM_LIMIT)


def _mod_row(row0, n_prompt_rows, dec_seq):
    return jnp.maximum(row0 + (dec_seq - n_prompt_rows), 0) // dec_seq


def _ada_kernel(c_ref, w_ref, b_ref, o_ref):
    c = c_ref[...]
    s = (c * jax.nn.sigmoid(c)).astype(BF16)
    o_ref[0] = jnp.dot(s, w_ref[0].astype(BF16), preferred_element_type=F32) + b_ref[0]


def _ada(cond8, ada_w, ada_b):
    depth, d, n = ada_w.shape
    tn = 1024
    return pl.pallas_call(
        _ada_kernel,
        grid=(depth, n // tn),
        in_specs=[pl.BlockSpec((SUBLANES, d), lambda l, j: (0, 0)),
                  pl.BlockSpec((1, d, tn), lambda l, j: (l, 0, j)),
                  pl.BlockSpec((1, 1, tn), lambda l, j: (l, 0, j))],
        out_specs=pl.BlockSpec((1, SUBLANES, tn), lambda l, j: (l, 0, j)),
        out_shape=jax.ShapeDtypeStruct((depth, SUBLANES, n), F32),
        compiler_params=_params("parallel", "parallel"),
        name="ada_modulation",
    )(cond8, ada_w, ada_b.reshape(depth, 1, n))


def _rms(x):
    return x * lax.rsqrt(jnp.mean(x * x, axis=-1, keepdims=True) + EPS)


def _norm_mm_kernel(x_ref, g_ref, sh_ref, sc_ref, w_ref, o_ref, h_ref, *, relu2):
    @pl.when(pl.program_id(1) == 0)
    def _():
        y = _rms(x_ref[...]) * g_ref[...]
        h_ref[...] = (y * (1.0 + sc_ref[0, 0]) + sh_ref[0, 0]).astype(BF16)

    acc = jnp.dot(h_ref[...], w_ref[0], preferred_element_type=F32)
    if relu2:
        acc = jnp.maximum(acc, 0.0)
        acc = acc * acc
    o_ref[...] = acc.astype(o_ref.dtype)


def _norm_mm(x, g, mods, layer, chunk, w_stack, w_layer, *, relu2, out_dtype, tm, tn,
             n_prompt_rows, dec_seq):
    m, d = x.shape
    n = w_stack.shape[2]
    row = lambda i: _mod_row(i * tm, n_prompt_rows, dec_seq)
    return pl.pallas_call(
        functools.partial(_norm_mm_kernel, relu2=relu2),
        grid=(m // tm, n // tn),
        in_specs=[pl.BlockSpec((tm, d), lambda i, j: (i, 0)),
                  pl.BlockSpec((1, d), lambda i, j: (0, 0)),
                  pl.BlockSpec((1, 1, 1, d), lambda i, j: (layer, row(i), 0, chunk)),
                  pl.BlockSpec((1, 1, 1, d), lambda i, j: (layer, row(i), 0, chunk + 1)),
                  pl.BlockSpec((1, d, tn), lambda i, j: (w_layer, 0, j))],
        out_specs=pl.BlockSpec((tm, tn), lambda i, j: (i, j)),
        out_shape=jax.ShapeDtypeStruct((m, n), out_dtype),
        scratch_shapes=[pltpu.VMEM((tm, d), BF16)],
        compiler_params=_params("parallel", "arbitrary"),
        name="norm_mod_matmul",
    )(x, g.reshape(1, d), mods, mods, w_stack)


def _mm_res_kernel(*refs, n_parts, nk):
    a_refs = refs[:n_parts]
    w_refs = refs[n_parts:2 * n_parts]
    res_ref, gate_ref, o_ref = refs[2 * n_parts:2 * n_parts + 3]
    part = jnp.dot(a_refs[0][...], w_refs[0][0], preferred_element_type=F32)
    for a_ref, w_ref in zip(a_refs[1:], w_refs[1:]):
        part = part + jnp.dot(a_ref[...], w_ref[0], preferred_element_type=F32)
    if nk == 1:
        o_ref[...] = res_ref[...] + gate_ref[0, 0] * part
        return
    acc_ref = refs[-1]
    k = pl.program_id(2)

    @pl.when(k == 0)
    def _():
        acc_ref[...] = part

    @pl.when(k > 0)
    def _():
        acc_ref[...] += part

    @pl.when(k == nk - 1)
    def _():
        o_ref[...] = res_ref[...] + gate_ref[0, 0] * acc_ref[...]


def _mm_res(a_parts, w_stack, w_layer, res, mods, layer, chunk, *, tm, tn, tk, n_prompt_rows,
            dec_seq):
    m, n = res.shape
    n_parts = len(a_parts)
    kdim = a_parts[0].shape[1]
    nk = kdim // tk
    gate_blk = chunk * (n // tn)
    row = lambda i: _mod_row(i * tm, n_prompt_rows, dec_seq)
    w_spec = lambda p: pl.BlockSpec((1, tk, tn), lambda i, j, k: (w_layer, p * nk + k, j))
    in_specs = ([pl.BlockSpec((tm, tk), lambda i, j, k: (i, k)) for _ in a_parts]
                + [w_spec(p) for p in range(n_parts)]
                + [pl.BlockSpec((tm, tn), lambda i, j, k: (i, j)),
                   pl.BlockSpec((1, 1, 1, tn), lambda i, j, k: (layer, row(i), 0, gate_blk + j))])
    scratch = [pltpu.VMEM((tm, tn), F32)] if nk > 1 else []
    return pl.pallas_call(
        functools.partial(_mm_res_kernel, n_parts=n_parts, nk=nk),
        grid=(m // tm, n // tn, nk),
        in_specs=in_specs,
        out_specs=pl.BlockSpec((tm, tn), lambda i, j, k: (i, j)),
        out_shape=jax.ShapeDtypeStruct((m, n), F32),
        scratch_shapes=scratch,
        compiler_params=_params("parallel", "parallel", "arbitrary"),
        name="matmul_gated_residual",
    )(*a_parts, *([w_stack] * n_parts), res, mods)


def _softmax(s):
    e = jnp.exp(s - jnp.max(s, axis=-1, keepdims=True))
    return e / jnp.sum(e, axis=-1, keepdims=True)


_QK_DIMS = (((1,), (1,)), ((), ()))


def _rope(x, cos, sin_up, sin_dn, quarter):
    width = x.shape[-1]
    return (x * cos + pltpu.roll(x, width - quarter, axis=1) * sin_up
            + pltpu.roll(x, quarter, axis=1) * sin_dn)


def _rope_tables(n_tokens, dim, reps):
    rows = n_tokens // GRID_W
    row = jnp.repeat(jnp.arange(rows, dtype=F32), GRID_W)
    col = jnp.tile(jnp.arange(GRID_W, dtype=F32), rows)
    quarter = dim // 4
    inv_freq = ROPE_BASE ** (-jnp.arange(quarter, dtype=F32) / quarter)
    ang_r = row[:, None] * inv_freq[None, :]
    ang_c = col[:, None] * inv_freq[None, :]
    ang = jnp.concatenate([ang_r, ang_r, ang_c, ang_c], axis=-1)
    cos, sin = jnp.cos(ang), jnp.sin(ang)
    even_chunk = ((jnp.arange(dim) // quarter) % 2 == 0)[None, :]
    sin_up = jnp.where(even_chunk, -sin, 0.0)
    sin_dn = jnp.where(even_chunk, 0.0, sin)
    tile = lambda t: jnp.tile(t, (1, reps))
    return tile(cos), tile(sin_up), tile(sin_dn)


def _diff_lambda(lam_ref, lam_init):
    lp = lam_ref[...]
    a = jnp.sum(lp[0:1] * lp[1:2], axis=-1, keepdims=True)
    b = jnp.sum(lp[2:3] * lp[3:4], axis=-1, keepdims=True)
    return jnp.exp(a) - jnp.exp(b) + lam_init


def _diff_attend(q, kb, vb, lam, subln, lam_init):
    lane = lax.broadcasted_iota(jnp.int32, q.shape, 1)
    q1 = jnp.where(lane < A_DK, q, 0.0).astype(BF16)
    q2 = jnp.where(lane >= A_DK, q, 0.0).astype(BF16)
    scale = A_DK ** -0.5
    s1 = lax.dot_general(q1, kb, _QK_DIMS, preferred_element_type=F32) * scale
    s2 = lax.dot_general(q2, kb, _QK_DIMS, preferred_element_type=F32) * scale
    p = _softmax(s1) - lam * _softmax(s2)
    o = jnp.dot(p.astype(BF16), vb, preferred_element_type=F32)
    return _rms(o) * subln * (1.0 - lam_init)


def _dattn_prompt_kernel(q_ref, k_ref, v_ref, lam_ref, sub_ref, *refs, lam_init):
    o_ref, ko_ref, vo_ref = refs[-3:]
    lam = _diff_lambda(lam_ref, lam_init)
    for h in range(A_HEADS):
        cols = slice(h * A_DV, (h + 1) * A_DV)
        k = k_ref[:, cols]
        v = v_ref[:, cols]
        ko_ref[0, 0, h] = k
        vo_ref[0, 0, h] = v
        o = _diff_attend(q_ref[:, cols], k.astype(BF16), v.astype(BF16), lam, sub_ref[...],
                         lam_init)
        o_ref[:, cols] = o.astype(o_ref.dtype)


def _dattn_prompt(proj, lam_p, subln, lam_init, e, n_even, kv_prev, *, batch, seq, n_rows):
    hd = 2 * A_DK
    width = A_HEADS * hd
    kv_shape = jax.ShapeDtypeStruct((batch, n_even, A_HEADS, seq, hd), F32)
    kv_spec = pl.BlockSpec((1, 1, A_HEADS, seq, hd), lambda b: (b, e, 0, 0, 0))
    extra, extra_specs, aliases = [], [], {}
    if kv_prev is not None:
        extra = list(kv_prev)
        extra_specs = [pl.BlockSpec(memory_space=pl.ANY)] * 2
        aliases = {5: 1, 6: 2}
    return pl.pallas_call(
        functools.partial(_dattn_prompt_kernel, lam_init=lam_init),
        grid=(batch,),
        in_specs=[pl.BlockSpec((seq, width), lambda b: (b, 0)),
                  pl.BlockSpec((seq, width), lambda b: (b, 1)),
                  pl.BlockSpec((seq, width), lambda b: (b, 2)),
                  pl.BlockSpec(lam_p.shape, lambda b: (0, 0)),
                  pl.BlockSpec((1, hd), lambda b: (0, 0))] + extra_specs,
        out_specs=[pl.BlockSpec((seq, width), lambda b: (b, 0)), kv_spec, kv_spec],
        out_shape=[jax.ShapeDtypeStruct((n_rows, width), BF16), kv_shape, kv_shape],
        input_output_aliases=aliases,
        compiler_params=_params("parallel"),
        name="diff_attention_prompt",
    )(proj, proj, proj, lam_p, subln.reshape(1, hd), *extra)


def _dattn_sample_kernel(q_ref, k_ref, v_ref, ck_ref, cv_ref, cos_ref, su_ref, sd_ref, lam_ref,
                         sub_ref, att_in_ref, o_ref, kf_ref, vf_ref, *, lam_init, tq):
    del att_in_ref
    t = k_ref.shape[0]
    quarter = A_DK // 4
    kf_ref[0:t] = _rope(k_ref[...], cos_ref[...], su_ref[...], sd_ref[...], quarter).astype(BF16)
    kf_ref[t:] = ck_ref[0, 0, 0].astype(BF16)
    vf_ref[0:t] = v_ref[...].astype(BF16)
    vf_ref[t:] = cv_ref[0, 0, 0].astype(BF16)
    lam = _diff_lambda(lam_ref, lam_init)
    for i in range(t // tq):
        rows = slice(i * tq, (i + 1) * tq)
        q = _rope(q_ref[rows, :], cos_ref[rows, :], su_ref[rows, :], sd_ref[rows, :], quarter)
        o = _diff_attend(q, kf_ref[...], vf_ref[...], lam, sub_ref[...], lam_init)
        o_ref[rows, :] = o.astype(o_ref.dtype)


def _dattn_sample(proj, cache_k, cache_v, e, rope, lam_p, subln, lam_init, att, *,
                  dec_batch, dec_seq, n_prompt_rows):
    hd = 2 * A_DK
    past = cache_k.shape[3]
    rb = n_prompt_rows // dec_seq
    cache_spec = pl.BlockSpec((1, 1, 1, past, hd), lambda b, h: (b, e, h, 0, 0))
    tab_spec = pl.BlockSpec((dec_seq, hd), lambda b, h: (0, 0))
    return pl.pallas_call(
        functools.partial(_dattn_sample_kernel, lam_init=lam_init, tq=256),
        grid=(dec_batch, A_HEADS),
        in_specs=[pl.BlockSpec((dec_seq, hd), lambda b, h: (rb + b, h)),
                  pl.BlockSpec((dec_seq, hd), lambda b, h: (rb + b, A_HEADS + h)),
                  pl.BlockSpec((dec_seq, hd), lambda b, h: (rb + b, 2 * A_HEADS + h)),
                  cache_spec, cache_spec, tab_spec, tab_spec, tab_spec,
                  pl.BlockSpec(lam_p.shape, lambda b, h: (0, 0)),
                  pl.BlockSpec((1, hd), lambda b, h: (0, 0)),
                  pl.BlockSpec(memory_space=pl.ANY)],
        out_specs=pl.BlockSpec((dec_seq, hd), lambda b, h: (rb + b, h)),
        out_shape=jax.ShapeDtypeStruct(att.shape, att.dtype),
        scratch_shapes=[pltpu.VMEM((dec_seq + past, hd), BF16),
                        pltpu.VMEM((dec_seq + past, hd), BF16)],
        input_output_aliases={10: 0},
        compiler_params=_params("parallel", "parallel"),
        name="diff_attention_sample",
    )(proj, proj, proj, cache_k, cache_v, *rope, lam_p, subln.reshape(1, hd), att)


def _gqa_attend(q, kb, vb):
    s = lax.dot_general(q.astype(BF16), kb, _QK_DIMS, preferred_element_type=F32) * (D_HEAD_DIM ** -0.5)
    return jnp.dot(_softmax(s).astype(BF16), vb, preferred_element_type=F32)


def _gqa_prompt_kernel(q_ref, k_ref, v_ref, g_ref, *refs):
    o_ref, ko_ref, vo_ref = refs[-3:]
    gq = g_ref[0:1]
    gk = g_ref[1:2]
    for g in range(D_KV_HEADS):
        kv_cols = slice(g * D_HEAD_DIM, (g + 1) * D_HEAD_DIM)
        k = _rms(k_ref[:, kv_cols]) * gk
        v = v_ref[:, kv_cols]
        ko_ref[0, 0, g] = k
        vo_ref[0, 0, g] = v
        kb = k.astype(BF16)
        vb = v.astype(BF16)
        for r in range(D_REP):
            head = g * D_REP + r
            cols = slice(head * D_HEAD_DIM, (head + 1) * D_HEAD_DIM)
            q = _rms(q_ref[:, cols]) * gq
            o_ref[:, cols] = _gqa_attend(q, kb, vb).astype(o_ref.dtype)


def _gqa_prompt(proj, qk_g, o, n_odd, kv_prev, *, batch, seq, n_rows, c_width):
    hd = D_HEAD_DIM
    qw = D_HEADS * hd
    kvw = D_KV_HEADS * hd
    q0 = c_width // qw
    k0 = (c_width + qw) // kvw
    kv_shape = jax.ShapeDtypeStruct((batch, n_odd, D_KV_HEADS, seq, hd), F32)
    kv_spec = pl.BlockSpec((1, 1, D_KV_HEADS, seq, hd), lambda b: (b, o, 0, 0, 0))
    extra, extra_specs, aliases = [], [], {}
    if kv_prev is not None:
        extra = list(kv_prev)
        extra_specs = [pl.BlockSpec(memory_space=pl.ANY)] * 2
        aliases = {4: 1, 5: 2}
    return pl.pallas_call(
        _gqa_prompt_kernel,
        grid=(batch,),
        in_specs=[pl.BlockSpec((seq, qw), lambda b: (b, q0)),
                  pl.BlockSpec((seq, kvw), lambda b: (b, k0)),
                  pl.BlockSpec((seq, kvw), lambda b: (b, k0 + 1)),
                  pl.BlockSpec(qk_g.shape, lambda b: (0, 0))] + extra_specs,
        out_specs=[pl.BlockSpec((seq, qw), lambda b: (b, 0)), kv_spec, kv_spec],
        out_shape=[jax.ShapeDtypeStruct((n_rows, qw), BF16), kv_shape, kv_shape],
        input_output_aliases=aliases,
        compiler_params=_params("parallel"),
        name="gqa_prompt",
    )(proj, proj, proj, qk_g, *extra)


def _gqa_sample_kernel(q_ref, k_ref, v_ref, ck_ref, cv_ref, cos_ref, su_ref, sd_ref, g_ref,
                       att_in_ref, o_ref, kf_ref, vf_ref, *, tq):
    del att_in_ref
    t = k_ref.shape[0]
    quarter = D_HEAD_DIM // 4
    gq = g_ref[0:1]
    k = _rms(k_ref[...]) * g_ref[1:2]
    kf_ref[0:t] = _rope(k, cos_ref[...], su_ref[...], sd_ref[...], quarter).astype(BF16)
    kf_ref[t:] = ck_ref[0, 0, 0].astype(BF16)
    vf_ref[0:t] = v_ref[...].astype(BF16)
    vf_ref[t:] = cv_ref[0, 0, 0].astype(BF16)
    for r in range(D_REP):
        cols = slice(r * D_HEAD_DIM, (r + 1) * D_HEAD_DIM)
        for i in range(t // tq):
            rows = slice(i * tq, (i + 1) * tq)
            q = _rms(q_ref[rows, cols]) * gq
            q = _rope(q, cos_ref[rows, :], su_ref[rows, :], sd_ref[rows, :], quarter)
            o_ref[rows, cols] = _gqa_attend(q, kf_ref[...], vf_ref[...]).astype(o_ref.dtype)


def _gqa_sample(proj, cache_k, cache_v, o, rope, qk_g, att, *, dec_batch, dec_seq, n_prompt_rows,
                c_width):
    hd = D_HEAD_DIM
    qw = D_REP * hd
    q0 = c_width // qw
    k0 = (c_width + D_HEADS * hd) // hd
    v0 = k0 + D_KV_HEADS
    past = cache_k.shape[3]
    rb = n_prompt_rows // dec_seq
    cache_spec = pl.BlockSpec((1, 1, 1, past, hd), lambda b, g: (b, o, g, 0, 0))
    tab_spec = pl.BlockSpec((dec_seq, hd), lambda b, g: (0, 0))
    return pl.pallas_call(
        functools.partial(_gqa_sample_kernel, tq=256),
        grid=(dec_batch, D_KV_HEADS),
        in_specs=[pl.BlockSpec((dec_seq, qw), lambda b, g: (rb + b, q0 + g)),
                  pl.BlockSpec((dec_seq, hd), lambda b, g: (rb + b, k0 + g)),
                  pl.BlockSpec((dec_seq, hd), lambda b, g: (rb + b, v0 + g)),
                  cache_spec, cache_spec, tab_spec, tab_spec, tab_spec,
                  pl.BlockSpec(qk_g.shape, lambda b, g: (0, 0)),
                  pl.BlockSpec(memory_space=pl.ANY)],
        out_specs=pl.BlockSpec((dec_seq, qw), lambda b, g: (rb + b, g)),
        out_shape=jax.ShapeDtypeStruct(att.shape, att.dtype),
        scratch_shapes=[pltpu.VMEM((dec_seq + past, hd), BF16),
                        pltpu.VMEM((dec_seq + past, hd), BF16)],
        input_output_aliases={9: 0},
        compiler_params=_params("parallel", "parallel"),
        name="gqa_sample",
    )(proj, proj, proj, cache_k, cache_v, *rope, qk_g, att)


CONV_ROWS = 256
CONV_HALO = 16


def _conv_kernel(a_ref, g_ref, ap_ref, gp_ref, an_ref, gn_ref, w_ref, b_ref, lng_ref, lnb_ref,
                 o_ref, pad_ref, cv_ref, *, n_prompt_blocks, blocks_per_seq):
    rows, width = a_ref.shape
    n_chunks = width // LANES
    i = pl.program_id(0)
    j = jnp.maximum(i - n_prompt_blocks, 0) % blocks_per_seq
    latent = i >= n_prompt_blocks
    has_prev = jnp.where(jnp.logical_and(latent, j > 0), 1.0, 0.0)
    has_next = jnp.where(jnp.logical_and(latent, j < blocks_per_seq - 1), 1.0, 0.0)

    glu = lambda a, g: a * jax.nn.sigmoid(g)
    h_prev = glu(ap_ref[...], gp_ref[...]) * has_prev
    h_cur = glu(a_ref[...], g_ref[...])
    h_next = glu(an_ref[...], gn_ref[...]) * has_next
    for c in range(n_chunks):
        lanes = slice(c * LANES, (c + 1) * LANES)
        pad_ref[c, 0:CONV_HALO, :] = h_prev[:, lanes]
        pad_ref[c, CONV_HALO:CONV_HALO + rows, :] = h_cur[:, lanes]
        pad_ref[c, CONV_HALO + rows:, :] = h_next[:, lanes]

    def chunk_body(c, carry):
        for r in range(rows // SUBLANES):
            base = CONV_HALO - B_PAD + r * SUBLANES
            acc = w_ref[c, 0] * pad_ref[c, pl.ds(base, SUBLANES), :]
            for tap in range(1, B_KERNEL):
                acc = acc + w_ref[c, tap] * pad_ref[c, pl.ds(base + tap, SUBLANES), :]
            cv_ref[c, pl.ds(r * SUBLANES, SUBLANES), :] = acc
        return carry

    lax.fori_loop(0, n_chunks, chunk_body, 0)

    x = jnp.concatenate([cv_ref[c] for c in range(n_chunks)], axis=1) + b_ref[...]
    mu = jnp.mean(x, axis=-1, keepdims=True)
    xc = x - mu
    var = jnp.mean(xc * xc, axis=-1, keepdims=True)
    y = xc * lax.rsqrt(var + EPS) * lng_ref[...] + lnb_ref[...]
    o_ref[...] = (y * jax.nn.sigmoid(y)).astype(o_ref.dtype)


def _conv(proj, conv_w, conv_b, conv_ln, *, col0, n_prompt_rows, prompt_seq, dec_seq):
    m = proj.shape[0]
    taps, width = conv_w.shape
    assert prompt_seq == CONV_ROWS and dec_seq % CONV_ROWS == 0
    n_chunks = width // LANES
    a_blk = col0 // width
    halo_per_blk = CONV_ROWS // CONV_HALO
    last_halo = m // CONV_HALO - 1
    prev_map = lambda cb: (lambda i: (jnp.maximum(i * halo_per_blk - 1, 0), cb))
    next_map = lambda cb: (lambda i: (jnp.minimum((i + 1) * halo_per_blk, last_halo), cb))
    w8 = jnp.broadcast_to(conv_w.reshape(taps, 1, n_chunks, LANES), (taps, SUBLANES, n_chunks, LANES))
    w8 = w8.transpose(2, 0, 1, 3)
    row_spec = lambda: pl.BlockSpec((1, width), lambda i: (0, 0))
    return pl.pallas_call(
        functools.partial(_conv_kernel, n_prompt_blocks=n_prompt_rows // CONV_ROWS,
                          blocks_per_seq=dec_seq // CONV_ROWS),
        grid=(m // CONV_ROWS,),
        in_specs=[pl.BlockSpec((CONV_ROWS, width), lambda i: (i, a_blk)),
                  pl.BlockSpec((CONV_ROWS, width), lambda i: (i, a_blk + 1)),
                  pl.BlockSpec((CONV_HALO, width), prev_map(a_blk)),
                  pl.BlockSpec((CONV_HALO, width), prev_map(a_blk + 1)),
                  pl.BlockSpec((CONV_HALO, width), next_map(a_blk)),
                  pl.BlockSpec((CONV_HALO, width), next_map(a_blk + 1)),
                  pl.BlockSpec(w8.shape, lambda i: (0, 0, 0, 0)),
                  row_spec(), row_spec(), row_spec()],
        out_specs=pl.BlockSpec((CONV_ROWS, width), lambda i: (i, 0)),
        out_shape=jax.ShapeDtypeStruct((m, width), BF16),
        scratch_shapes=[pltpu.VMEM((n_chunks, CONV_ROWS + 2 * CONV_HALO, LANES), F32),
                        pltpu.VMEM((n_chunks, CONV_ROWS, LANES), F32)],
        compiler_params=_params("parallel"),
        name="conformer_conv",
    )(proj, proj, proj, proj, proj, proj, w8, conv_b.reshape(1, width),
      conv_ln[0].reshape(1, width), conv_ln[1].reshape(1, width))


def _s5_lag_kernel(ar_ref, ai_ref, ldt_ref, btr_ref, bti_ref, cr_ref, ci_ref,
                   klag_ref, wtr_ref, wti_ref, vr_ref, vi_ref, alr_ref, ali_ref):
    ar = ar_ref[0]
    ai = ai_ref[0]
    dt = jnp.exp(ldt_ref[0])
    mag = jnp.exp(ar * dt)
    abr = mag * jnp.cos(ai * dt)
    abi = mag * jnp.sin(ai * dt)
    den = ar * ar + ai * ai
    nr = abr - 1.0
    ni = abi
    kr = (nr * ar + ni * ai) / den
    ki = (ni * ar - nr * ai) / den
    al_r, al_i = [], []
    for gi in range(ar.shape[0]):
        row = slice(gi, gi + 1)
        a_r, a_i = abr[row], abi[row]
        bbr = kr[row] * btr_ref[0, gi] - ki[row] * bti_ref[0, gi]
        bbi = kr[row] * bti_ref[0, gi] + ki[row] * btr_ref[0, gi]
        cr = cr_ref[0, gi]
        ci = ci_ref[0, gi]
        pr, pi = jnp.ones_like(a_r), jnp.zeros_like(a_r)
        wt_r, wt_i, v_r, v_i = [], [], [], []
        for lag in range(S5_CHUNK + 1):
            if lag < S5_CHUNK:
                wt_r.append(pr * bbr - pi * bbi)
                wt_i.append(pr * bbi + pi * bbr)
            if lag >= 1:
                v_r.append(cr * pr - ci * pi)
                v_i.append(cr * pi + ci * pr)
            if lag < S5_CHUNK:
                pr, pi = pr * a_r - pi * a_i, pr * a_i + pi * a_r
        al_r.append(pr)
        al_i.append(pi)
        wtr = jnp.concatenate(wt_r, axis=0)
        wti = jnp.concatenate(wt_i, axis=0)
        wtr_ref[0, gi] = wtr
        wti_ref[0, gi] = wti
        vr_ref[0, gi] = jnp.concatenate(v_r, axis=0)
        vi_ref[0, gi] = jnp.concatenate(v_i, axis=0)
        hi = lax.Precision.HIGHEST
        klag_ref[0, gi] = (
            lax.dot_general(cr, wtr, _QK_DIMS, precision=hi, preferred_element_type=F32)
            - lax.dot_general(ci, wti, _QK_DIMS, precision=hi, preferred_element_type=F32))
    alr_ref[0] = jnp.concatenate(al_r, axis=0)
    ali_ref[0] = jnp.concatenate(al_i, axis=0)


def _s5_lag_tables(a_re, a_im, log_dt, b, c):
    nd, g, p = a_re.shape
    h = b.shape[-1]
    gb = S5_LAG_GROUPS_PER_STEP
    rows = S5_CHUNK * h
    bt = b.transpose(0, 1, 2, 4, 3)
    ldt = jnp.broadcast_to(log_dt[:, :, None], (nd, g, p))
    vec = pl.BlockSpec((1, gb, p), lambda d, i: (d, i, 0))
    mat = pl.BlockSpec((1, gb, h, p), lambda d, i: (d, i, 0, 0))
    tall = pl.BlockSpec((1, gb, rows, p), lambda d, i: (d, i, 0, 0))
    tall_shape = jax.ShapeDtypeStruct((nd, g, rows, p), F32)
    vec_shape = jax.ShapeDtypeStruct((nd, g, p), F32)
    return pl.pallas_call(
        _s5_lag_kernel,
        grid=(nd, g // gb),
        in_specs=[vec, vec, vec, mat, mat, mat, mat],
        out_specs=[pl.BlockSpec((1, gb, h, rows), lambda d, i: (d, i, 0, 0)),
                   tall, tall, tall, tall, vec, vec],
        out_shape=[jax.ShapeDtypeStruct((nd, g, h, rows), F32),
                   tall_shape, tall_shape, tall_shape, tall_shape, vec_shape, vec_shape],
        compiler_params=_params("parallel", "parallel"),
        name="s5_lag_tables",
    )(a_re, a_im, ldt, bt[:, 0], bt[:, 1], c[:, 0], c[:, 1])


def _s5_chunk_weights(klag, wtr, wti, vr, vi, alr, ali):
    nd, g, h, _ = klag.shape
    p = wtr.shape[-1]
    n_lag = S5_CHUNK
    k5 = klag.reshape(nd, g, h, n_lag, h)
    s_idx = jnp.arange(n_lag)[:, None]
    t_idx = jnp.arange(n_lag)[None, :]
    kf = jnp.where((t_idx >= s_idx)[None, None, :, :, None],
                   k5[0][:, :, jnp.maximum(t_idx - s_idx, 0), :], 0.0)
    kb = jnp.where((s_idx >= t_idx)[None, None, :, :, None],
                   k5[1][:, :, jnp.maximum(s_idx - t_idx, 0), :], 0.0)
    m_w = (kf + kb).transpose(0, 2, 4, 3, 1).reshape(g, n_lag * h, n_lag * h)
    flip = lambda w: w.reshape(g, n_lag, h, p)[:, ::-1].reshape(g, n_lag * h, p)
    p_w = jnp.concatenate([flip(wtr[0]), wtr[1], flip(wti[0]), wti[1]], axis=-1)
    to_q = lambda v: v.reshape(g, n_lag, h, p).transpose(0, 3, 1, 2).reshape(g, p, n_lag * h)
    flip_q = lambda v: to_q(v.reshape(g, n_lag, h, p)[:, ::-1].reshape(g, n_lag * h, p))
    q_w = jnp.concatenate([to_q(vr[0]), flip_q(vr[1]), -to_q(vi[0]), -flip_q(vi[1])], axis=1)
    a_l = jnp.concatenate([alr[0], alr[1], ali[0], ali[1]], axis=-1).reshape(g, 1, 4 * p)
    return m_w.astype(BF16), p_w.astype(BF16), q_w.astype(BF16), a_l


def _s5_chunk_kernel(u_ref, m_ref, p_ref, q_ref, al_ref, h0_ref, y_ref, hf_ref, x_ref, sa_ref,
                     sd_ref, *, p_rows, p_steps, s_rows, s_steps):
    half = x_ref.shape[1] // 2
    lane = lax.broadcasted_iota(jnp.int32, (1, 2 * half), 1)
    fwd = (lane % half) < (half // 2)

    def scan(out_ref, gi, row0, rows, steps, sr, si, reverse):
        alr = al_ref[gi, :, 0:half]
        ali = al_ref[gi, :, half:]
        for c in (range(steps - 1, -1, -1) if reverse else range(steps)):
            rs = slice(row0 + c * rows, row0 + (c + 1) * rows)
            out_ref[rs, 0:half] = sr
            out_ref[rs, half:] = si
            xr = x_ref[rs, 0:half]
            xi = x_ref[rs, half:]
            sr, si = alr * sr - ali * si + xr, alr * si + ali * sr + xi
        return sr, si

    def group(gi, carry):
        u = u_ref[gi]
        x_ref[...] = jnp.dot(u, p_ref[gi], preferred_element_type=F32)
        zero = jnp.zeros((p_rows, half), F32)
        fa = scan(sa_ref, gi, 0, p_rows, p_steps, zero, zero, False)
        fd = scan(sd_ref, gi, 0, p_rows, p_steps, zero, zero, True)
        h0r = h0_ref[gi, :, 0:half]
        h0i = h0_ref[gi, :, half:]
        s_row0 = p_rows * p_steps
        scan(sa_ref, gi, s_row0, s_rows, s_steps, h0r, h0i, False)
        scan(sd_ref, gi, s_row0, s_rows, s_steps, h0r, h0i, True)
        s = jnp.where(fwd, sa_ref[...], sd_ref[...])
        y_ref[gi] = (jnp.dot(u, m_ref[gi], preferred_element_type=F32)
                     + jnp.dot(s.astype(BF16), q_ref[gi], preferred_element_type=F32))
        fwd_half = fwd[:, 0:half]
        hf_ref[gi, :, 0:half] = jnp.where(fwd_half, fa[0], fd[0])
        hf_ref[gi, :, half:] = jnp.where(fwd_half, fa[1], fd[1])
        return carry

    lax.fori_loop(0, u_ref.shape[0], group, 0)


def _s5_chunks(u_g, m_w, p_w, q_w, a_l, h0, *, p_rows, p_steps, s_rows, s_steps):
    g, rows, tile = u_g.shape
    gb = S5_GROUPS_PER_STEP
    blk = lambda r, c: pl.BlockSpec((gb, r, c), lambda i: (i, 0, 0))
    return pl.pallas_call(
        functools.partial(_s5_chunk_kernel, p_rows=p_rows, p_steps=p_steps, s_rows=s_rows,
                          s_steps=s_steps),
        grid=(g // gb,),
        in_specs=[blk(rows, tile), blk(tile, tile), blk(tile, tile), blk(tile, tile),
                  blk(1, tile), blk(s_rows, tile)],
        out_specs=[blk(rows, tile), blk(p_rows, tile)],
        out_shape=[jax.ShapeDtypeStruct((g, rows, tile), F32),
                   jax.ShapeDtypeStruct((g, p_rows, tile), F32)],
        scratch_shapes=[pltpu.VMEM((rows, tile), F32)] * 3,
        compiler_params=_params("parallel"),
        name="s5_chunks",
    )(u_g, m_w, p_w, q_w, a_l, h0)


def _s5_glu_kernel(u_ref, y_ref, d_ref, w_ref, b_ref, o_ref):
    y = jax.nn.gelu(d_ref[...] * u_ref[...] + y_ref[...])
    z = jnp.dot(y.astype(BF16), w_ref[0], preferred_element_type=F32) + b_ref[...]
    o_ref[...] = (y * jax.nn.sigmoid(z)).astype(o_ref.dtype)


def _s5_glu(proj, y, ssm_d, glu_w_stack, o, glu_b, *, tm):
    m = proj.shape[0]
    width = y.shape[1]
    return pl.pallas_call(
        _s5_glu_kernel,
        grid=(m // tm,),
        in_specs=[pl.BlockSpec((tm, width), lambda i: (i, 0)),
                  pl.BlockSpec((tm, width), lambda i: (i, 0)),
                  pl.BlockSpec((1, width), lambda i: (0, 0)),
                  pl.BlockSpec((1, width, width), lambda i: (o, 0, 0)),
                  pl.BlockSpec((1, width), lambda i: (0, 0))],
        out_specs=pl.BlockSpec((tm, width), lambda i: (i, 0)),
        out_shape=jax.ShapeDtypeStruct((m, width), BF16),
        compiler_params=_params("parallel"),
        name="s5_gelu_glu",
    )(proj, y, ssm_d.reshape(1, width), glu_w_stack, glu_b.reshape(1, width))


def _s5_mixer(proj, state0, a_re, a_im, log_dt, b, c, *, batch, seq, dec_batch, dec_seq):
    n_p = batch * seq
    nd, g, p = a_re.shape
    h = b.shape[-1]
    lc = S5_CHUNK
    m_w, p_w, q_w, a_l = _s5_chunk_weights(*_s5_lag_tables(a_re, a_im, log_dt, b, c))
    u = proj[:, :g * h].astype(BF16)
    def to_groups(x, nb, t):
        x = x.reshape(nb, t // lc, lc, g, h).transpose(3, 1, 0, 2, 4)
        return x.reshape(g, (t // lc) * nb, lc * h)

    def from_groups(y, nb, t):
        y = y.reshape(g, t // lc, nb, lc, h).transpose(2, 1, 3, 0, 4)
        return y.reshape(nb * t, g * h)

    u_g = jnp.concatenate([to_groups(u[:n_p], batch, seq), to_groups(u[n_p:], dec_batch, dec_seq)],
                          axis=1)
    h0 = state0.transpose(3, 0, 2, 1, 4).reshape(g, dec_batch, 4 * p)
    y_g, hf = _s5_chunks(u_g, m_w, p_w, q_w, a_l, h0, p_rows=batch, p_steps=seq // lc,
                         s_rows=dec_batch, s_steps=dec_seq // lc)
    n_pc = batch * (seq // lc)
    y = jnp.concatenate([from_groups(y_g[:, :n_pc], batch, seq),
                         from_groups(y_g[:, n_pc:], dec_batch, dec_seq)], axis=0)
    state = hf.reshape(g, batch, 2, 2, p).transpose(1, 3, 2, 0, 4)
    return y, state


def _final_norm_kernel(x_ref, g_ref, o_ref):
    o_ref[...] = _rms(x_ref[...]) * g_ref[...]


def _final_norm(x, g, *, row0, n_rows, tm):
    d = x.shape[1]
    b0 = row0 // tm
    return pl.pallas_call(
        _final_norm_kernel,
        grid=(n_rows // tm,),
        in_specs=[pl.BlockSpec((tm, d), lambda i: (b0 + i, 0)),
                  pl.BlockSpec((1, d), lambda i: (0, 0))],
        out_specs=pl.BlockSpec((tm, d), lambda i: (i, 0)),
        out_shape=jax.ShapeDtypeStruct((n_rows, d), F32),
        compiler_params=_params("parallel"),
        name="final_rms_norm",
    )(x, g.reshape(1, d))


def kernel(x_prompt, x_sample, c, cache_a_k, cache_a_v, cache_d_k, cache_d_v, state_c_ssm, c_ctx,
           ada_w, ada_b, norm_g, mlp_w1, mlp_w2, even_w_in, even_w_out, diff_lambda, diff_subln,
           conv_w, conv_b, conv_ln, odd_w_in, odd_w_out, ssm_a_re, ssm_a_im, ssm_log_dt, ssm_b,
           ssm_c, ssm_d, ssm_glu_w, ssm_glu_b, qk_norm, final_norm):
    batch, seq, d_model = x_prompt.shape
    dec_batch, dec_seq, _ = x_sample.shape
    depth = ada_w.shape[0]
    n_even = even_w_in.shape[0]
    n_odd = odd_w_in.shape[0]
    n_p = batch * seq
    n_s = dec_batch * dec_seq
    n_rows = n_p + n_s
    rows_kw = dict(n_prompt_rows=n_p, dec_seq=dec_seq)
    a_qk = A_HEADS * 2 * A_DK
    a_width = A_HEADS * A_DV
    c_width = ssm_d.shape[1]
    assert dec_seq % seq == 0 and 1 + dec_batch <= SUBLANES

    x = jnp.concatenate([x_prompt.reshape(n_p, d_model), x_sample.reshape(n_s, d_model)], axis=0)
    cond8 = jnp.concatenate([c_ctx[None, :], c,
                             jnp.zeros((SUBLANES - 1 - dec_batch, d_model), F32)], axis=0)
    mods = _ada(cond8, ada_w, ada_b).reshape(depth, SUBLANES, 1, N_MOD * d_model)

    rope_a = _rope_tables(dec_seq, A_DK, 2)
    rope_d = _rope_tables(dec_seq, D_HEAD_DIM, 1)
    w1_b, w2_b = mlp_w1.astype(BF16), mlp_w2.astype(BF16)
    even_in_b, even_out_b = even_w_in.astype(BF16), even_w_out.astype(BF16)
    odd_in_b, odd_out_b = odd_w_in.astype(BF16), odd_w_out.astype(BF16)
    glu_w_b = ssm_glu_w.astype(BF16)

    a_kv, d_kv, c_st = None, None, []
    for l in range(depth):
        if l % 2 == 0:
            e = l // 2
            lam_init = 0.8 - 0.6 * math.exp(-0.3 * l)
            proj = _norm_mm(x, norm_g[l, 0], mods, l, 0, even_in_b, e, relu2=False,
                            out_dtype=F32, tm=1024, tn=512, **rows_kw)
            att, a_k, a_v = _dattn_prompt(proj, diff_lambda[e], diff_subln[e], lam_init, e, n_even,
                                          a_kv, batch=batch, seq=seq, n_rows=n_rows)
            a_kv = (a_k, a_v)
            att = _dattn_sample(proj, cache_a_k, cache_a_v, e, rope_a, diff_lambda[e],
                                diff_subln[e], lam_init, att, dec_batch=dec_batch, dec_seq=dec_seq,
                                n_prompt_rows=n_p)
            conv = _conv(proj, conv_w[e], conv_b[e], conv_ln[e], col0=2 * a_qk + a_width,
                         n_prompt_rows=n_p, prompt_seq=seq, dec_seq=dec_seq)
            x = _mm_res([att, conv], even_out_b, e, x, mods, l, 2,
                        tm=1024, tn=1024, tk=a_width, **rows_kw)
        else:
            o = l // 2
            proj = _norm_mm(x, norm_g[l, 0], mods, l, 0, odd_in_b, o, relu2=False,
                            out_dtype=F32, tm=1024, tn=512, **rows_kw)
            y, state = _s5_mixer(proj, state_c_ssm[:, o], ssm_a_re[o], ssm_a_im[o], ssm_log_dt[o],
                                 ssm_b[o], ssm_c[o], batch=batch, seq=seq, dec_batch=dec_batch,
                                 dec_seq=dec_seq)
            ssm_out = _s5_glu(proj, y, ssm_d[o], glu_w_b, o, ssm_glu_b[o], tm=1024)
            att, d_k, d_v = _gqa_prompt(proj, qk_norm[o], o, n_odd, d_kv, batch=batch, seq=seq,
                                        n_rows=n_rows, c_width=c_width)
            d_kv = (d_k, d_v)
            att = _gqa_sample(proj, cache_d_k, cache_d_v, o, rope_d, qk_norm[o], att,
                              dec_batch=dec_batch, dec_seq=dec_seq, n_prompt_rows=n_p,
                              c_width=c_width)
            x = _mm_res([ssm_out, att], odd_out_b, o, x, mods, l, 2,
                        tm=1024, tn=1024, tk=c_width, **rows_kw)
            c_st.append(state)
        hidden = _norm_mm(x, norm_g[l, 1], mods, l, 3, w1_b, l, relu2=True,
                          out_dtype=BF16, tm=1024, tn=1024, **rows_kw)
        x = _mm_res([hidden], w2_b, l, x, mods, l, 5, tm=1024, tn=1024, tk=2048, **rows_kw)

    y_prompt = _final_norm(x, final_norm, row0=0, n_rows=n_p, tm=512).reshape(batch, seq, d_model)
    y_sample = _final_norm(x, final_norm, row0=n_p, n_rows=n_s, tm=512)
    y_sample = y_sample.reshape(dec_batch, dec_seq, d_model)
    return (y_prompt, y_sample, a_kv[0], a_kv[1], d_kv[0], d_kv[1], jnp.stack(c_st, axis=1))
```

```python
import functools
import math

import jax
import jax.numpy as jnp
from jax import lax
from jax.experimental import pallas as pl
from jax.experimental.pallas import tpu as pltpu

F32 = jnp.float32
BF16 = jnp.bfloat16

EPS = 1e-6
ROPE_BASE = 10000.0
GRID_W = 64
N_MOD = 6
A_HEADS = 8
A_DK = 64
A_DV = 2 * A_DK
B_KERNEL = 31
B_PAD = (B_KERNEL - 1) // 2
C_GROUP = 16
C_STATE = 64
D_HEADS = 8
D_KV_HEADS = 2
D_REP = D_HEADS // D_KV_HEADS
D_HEAD_DIM = 128

LANES = 128
SUBLANES = 8
VMEM_LIMIT = 56 * 1024 * 1024
NORM_ROWS = 16

S5_CHUNK = 16
S5_TILE = S5_CHUNK * C_GROUP
S5_GROUPS_PER_STEP = 4
S5_LAG_GROUPS_PER_STEP = 8


def _params(*sem):
    return pltpu.CompilerParams(dimension_semantics=sem, vmem_limit_bytes=VMEM_LIMIT)


def _mod_row(row0, n_prompt_rows, dec_seq):
    return jnp.maximum(row0 + (dec_seq - n_prompt_rows), 0) // dec_seq


def _ada_kernel(c_ref, w_ref, b_ref, o_ref):
    c = c_ref[...]
    s = (c * jax.nn.sigmoid(c)).astype(BF16)
    o_ref[0] = jnp.dot(s, w_ref[0].astype(BF16), preferred_element_type=F32) + b_ref[0]


def _ada(cond8, ada_w, ada_b):
    depth, d, n = ada_w.shape
    tn = 1024
    return pl.pallas_call(
        _ada_kernel,
        grid=(depth, n // tn),
        in_specs=[pl.BlockSpec((SUBLANES, d), lambda l, j: (0, 0)),
                  pl.BlockSpec((1, d, tn), lambda l, j: (l, 0, j)),
                  pl.BlockSpec((1, 1, tn), lambda l, j: (l, 0, j))],
        out_specs=pl.BlockSpec((1, SUBLANES, tn), lambda l, j: (l, 0, j)),
        out_shape=jax.ShapeDtypeStruct((depth, SUBLANES, n), F32),
        compiler_params=_params("parallel", "parallel"),
        name="ada_modulation",
    )(cond8, ada_w, ada_b.reshape(depth, 1, n))


def _rms(x):
    return x * lax.rsqrt(jnp.mean(x * x, axis=-1, keepdims=True) + EPS)


def _norm_mm_kernel(x_ref, g_ref, sh_ref, sc_ref, w_ref, o_ref, h_ref, *, relu2):
    @pl.when(pl.program_id(1) == 0)
    def _():
        gain = g_ref[...] * (1.0 + sc_ref[0, 0])
        shift = sh_ref[0, 0]

        def rows_body(r, carry):
            rows = pl.ds(pl.multiple_of(r * NORM_ROWS, NORM_ROWS), NORM_ROWS)
            h_ref[rows, :] = (_rms(x_ref[rows, :]) * gain + shift).astype(BF16)
            return carry

        lax.fori_loop(0, x_ref.shape[0] // NORM_ROWS, rows_body, 0, unroll=4)

    acc = jnp.dot(h_ref[...], w_ref[0], preferred_element_type=F32)
    if relu2:
        acc = jnp.maximum(acc, 0.0)
        acc = acc * acc
    o_ref[...] = acc.astype(o_ref.dtype)


def _norm_mm(x, g, mods, layer, chunk, w_stack, w_layer, *, relu2, out_dtype, tm, tn,
             n_prompt_rows, dec_seq):
    m, d = x.shape
    n = w_stack.shape[2]
    row = lambda i: _mod_row(i * tm, n_prompt_rows, dec_seq)
    return pl.pallas_call(
        functools.partial(_norm_mm_kernel, relu2=relu2),
        grid=(m // tm, n // tn),
        in_specs=[pl.BlockSpec((tm, d), lambda i, j: (i, 0)),
                  pl.BlockSpec((1, d), lambda i, j: (0, 0)),
                  pl.BlockSpec((1, 1, 1, d), lambda i, j: (layer, row(i), 0, chunk)),
                  pl.BlockSpec((1, 1, 1, d), lambda i, j: (layer, row(i), 0, chunk + 1)),
                  pl.BlockSpec((1, d, tn), lambda i, j: (w_layer, 0, j))],
        out_specs=pl.BlockSpec((tm, tn), lambda i, j: (i, j)),
        out_shape=jax.ShapeDtypeStruct((m, n), out_dtype),
        scratch_shapes=[pltpu.VMEM((tm, d), BF16)],
        compiler_params=_params("parallel", "arbitrary"),
        name="norm_mod_matmul",
    )(x, g.reshape(1, d), mods, mods, w_stack)


def _mm_res_kernel(*refs, n_parts, nk):
    a_refs = refs[:n_parts]
    w_refs = refs[n_parts:2 * n_parts]
    res_ref, gate_ref, o_ref = refs[2 * n_parts:2 * n_parts + 3]
    part = jnp.dot(a_refs[0][...], w_refs[0][0], preferred_element_type=F32)
    for a_ref, w_ref in zip(a_refs[1:], w_refs[1:]):
        part = part + jnp.dot(a_ref[...], w_ref[0], preferred_element_type=F32)
    if nk == 1:
        o_ref[...] = res_ref[...] + gate_ref[0, 0] * part
        return
    acc_ref = refs[-1]
    k = pl.program_id(2)

    @pl.when(k == 0)
    def _():
        acc_ref[...] = part

    @pl.when(k > 0)
    def _():
        acc_ref[...] += part

    @pl.when(k == nk - 1)
    def _():
        o_ref[...] = res_ref[...] + gate_ref[0, 0] * acc_ref[...]


def _mm_res(a_parts, w_stack, w_layer, res, mods, layer, chunk, *, tm, tn, tk, n_prompt_rows,
            dec_seq):
    m, n = res.shape
    n_parts = len(a_parts)
    kdim = a_parts[0].shape[1]
    nk = kdim // tk
    gate_blk = chunk * (n // tn)
    row = lambda i: _mod_row(i * tm, n_prompt_rows, dec_seq)
    w_spec = lambda p: pl.BlockSpec((1, tk, tn), lambda i, j, k: (w_layer, p * nk + k, j))
    in_specs = ([pl.BlockSpec((tm, tk), lambda i, j, k: (i, k)) for _ in a_parts]
                + [w_spec(p) for p in range(n_parts)]
                + [pl.BlockSpec((tm, tn), lambda i, j, k: (i, j)),
                   pl.BlockSpec((1, 1, 1, tn), lambda i, j, k: (layer, row(i), 0, gate_blk + j))])
    scratch = [pltpu.VMEM((tm, tn), F32)] if nk > 1 else []
    return pl.pallas_call(
        functools.partial(_mm_res_kernel, n_parts=n_parts, nk=nk),
        grid=(m // tm, n // tn, nk),
        in_specs=in_specs,
        out_specs=pl.BlockSpec((tm, tn), lambda i, j, k: (i, j)),
        out_shape=jax.ShapeDtypeStruct((m, n), F32),
        scratch_shapes=scratch,
        compiler_params=_params("parallel", "parallel", "arbitrary"),
        name="matmul_gated_residual",
    )(*a_parts, *([w_stack] * n_parts), res, mods)


def _softmax(s):
    e = jnp.exp(s - jnp.max(s, axis=-1, keepdims=True))
    return e / jnp.sum(e, axis=-1, keepdims=True)


_QK_DIMS = (((1,), (1,)), ((), ()))


def _rope(x, cos, sin_up, sin_dn, quarter):
    width = x.shape[-1]
    return (x * cos + pltpu.roll(x, width - quarter, axis=1) * sin_up
            + pltpu.roll(x, quarter, axis=1) * sin_dn)


def _rope_tables(n_tokens, dim, reps):
    rows = n_tokens // GRID_W
    row = jnp.repeat(jnp.arange(rows, dtype=F32), GRID_W)
    col = jnp.tile(jnp.arange(GRID_W, dtype=F32), rows)
    quarter = dim // 4
    inv_freq = ROPE_BASE ** (-jnp.arange(quarter, dtype=F32) / quarter)
    ang_r = row[:, None] * inv_freq[None, :]
    ang_c = col[:, None] * inv_freq[None, :]
    ang = jnp.concatenate([ang_r, ang_r, ang_c, ang_c], axis=-1)
    cos, sin = jnp.cos(ang), jnp.sin(ang)
    even_chunk = ((jnp.arange(dim) // quarter) % 2 == 0)[None, :]
    sin_up = jnp.where(even_chunk, -sin, 0.0)
    sin_dn = jnp.where(even_chunk, 0.0, sin)
    tile = lambda t: jnp.tile(t, (1, reps))
    return tile(cos), tile(sin_up), tile(sin_dn)


def _diff_lambda(lam_ref, lam_init):
    lp = lam_ref[...]
    a = jnp.sum(lp[0:1] * lp[1:2], axis=-1, keepdims=True)
    b = jnp.sum(lp[2:3] * lp[3:4], axis=-1, keepdims=True)
    return jnp.exp(a) - jnp.exp(b) + lam_init


def _diff_attend(q, kb, vb, lam, subln, lam_init):
    lane = lax.broadcasted_iota(jnp.int32, q.shape, 1)
    q1 = jnp.where(lane < A_DK, q, 0.0).astype(BF16)
    q2 = jnp.where(lane >= A_DK, q, 0.0).astype(BF16)
    scale = A_DK ** -0.5
    s1 = lax.dot_general(q1, kb, _QK_DIMS, preferred_element_type=F32) * scale
    s2 = lax.dot_general(q2, kb, _QK_DIMS, preferred_element_type=F32) * scale
    p = _softmax(s1) - lam * _softmax(s2)
    o = jnp.dot(p.astype(BF16), vb, preferred_element_type=F32)
    return _rms(o) * subln * (1.0 - lam_init)


def _dattn_prompt_kernel(q_ref, k_ref, v_ref, lam_ref, sub_ref, *refs, lam_init):
    o_ref, ko_ref, vo_ref = refs[-3:]
    lam = _diff_lambda(lam_ref, lam_init)
    for h in range(A_HEADS):
        cols = slice(h * A_DV, (h + 1) * A_DV)
        k = k_ref[:, cols]
        v = v_ref[:, cols]
        ko_ref[0, 0, h] = k
        vo_ref[0, 0, h] = v
        o = _diff_attend(q_ref[:, cols], k.astype(BF16), v.astype(BF16), lam, sub_ref[...],
                         lam_init)
        o_ref[:, cols] = o.astype(o_ref.dtype)


def _dattn_prompt(proj, lam_p, subln, lam_init, e, n_even, kv_prev, *, batch, seq, n_rows):
    hd = 2 * A_DK
    width = A_HEADS * hd
    kv_shape = jax.ShapeDtypeStruct((batch, n_even, A_HEADS, seq, hd), F32)
    kv_spec = pl.BlockSpec((1, 1, A_HEADS, seq, hd), lambda b: (b, e, 0, 0, 0))
    extra, extra_specs, aliases = [], [], {}
    if kv_prev is not None:
        extra = list(kv_prev)
        extra_specs = [pl.BlockSpec(memory_space=pl.ANY)] * 2
        aliases = {5: 1, 6: 2}
    return pl.pallas_call(
        functools.partial(_dattn_prompt_kernel, lam_init=lam_init),
        grid=(batch,),
        in_specs=[pl.BlockSpec((seq, width), lambda b: (b, 0)),
                  pl.BlockSpec((seq, width), lambda b: (b, 1)),
                  pl.BlockSpec((seq, width), lambda b: (b, 2)),
                  pl.BlockSpec(lam_p.shape, lambda b: (0, 0)),
                  pl.BlockSpec((1, hd), lambda b: (0, 0))] + extra_specs,
        out_specs=[pl.BlockSpec((seq, width), lambda b: (b, 0)), kv_spec, kv_spec],
        out_shape=[jax.ShapeDtypeStruct((n_rows, width), BF16), kv_shape, kv_shape],
        input_output_aliases=aliases,
        compiler_params=_params("parallel"),
        name="diff_attention_prompt",
    )(proj, proj, proj, lam_p, subln.reshape(1, hd), *extra)


def _dattn_sample_kernel(q_ref, k_ref, v_ref, ck_ref, cv_ref, cos_ref, su_ref, sd_ref, lam_ref,
                         sub_ref, att_in_ref, o_ref, kf_ref, vf_ref, *, lam_init, tq):
    del att_in_ref
    t = k_ref.shape[0]
    quarter = A_DK // 4
    kf_ref[0:t] = _rope(k_ref[...], cos_ref[...], su_ref[...], sd_ref[...], quarter).astype(BF16)
    kf_ref[t:] = ck_ref[0, 0, 0].astype(BF16)
    vf_ref[0:t] = v_ref[...].astype(BF16)
    vf_ref[t:] = cv_ref[0, 0, 0].astype(BF16)
    lam = _diff_lambda(lam_ref, lam_init)
    for i in range(t // tq):
        rows = slice(i * tq, (i + 1) * tq)
        q = _rope(q_ref[rows, :], cos_ref[rows, :], su_ref[rows, :], sd_ref[rows, :], quarter)
        o = _diff_attend(q, kf_ref[...], vf_ref[...], lam, sub_ref[...], lam_init)
        o_ref[rows, :] = o.astype(o_ref.dtype)


def _dattn_sample(proj, cache_k, cache_v, e, rope, lam_p, subln, lam_init, att, *,
                  dec_batch, dec_seq, n_prompt_rows):
    hd = 2 * A_DK
    past = cache_k.shape[3]
    rb = n_prompt_rows // dec_seq
    cache_spec = pl.BlockSpec((1, 1, 1, past, hd), lambda b, h: (b, e, h, 0, 0))
    tab_spec = pl.BlockSpec((dec_seq, hd), lambda b, h: (0, 0))
    return pl.pallas_call(
        functools.partial(_dattn_sample_kernel, lam_init=lam_init, tq=256),
        grid=(dec_batch, A_HEADS),
        in_specs=[pl.BlockSpec((dec_seq, hd), lambda b, h: (rb + b, h)),
                  pl.BlockSpec((dec_seq, hd), lambda b, h: (rb + b, A_HEADS + h)),
                  pl.BlockSpec((dec_seq, hd), lambda b, h: (rb + b, 2 * A_HEADS + h)),
                  cache_spec, cache_spec, tab_spec, tab_spec, tab_spec,
                  pl.BlockSpec(lam_p.shape, lambda b, h: (0, 0)),
                  pl.BlockSpec((1, hd), lambda b, h: (0, 0)),
                  pl.BlockSpec(memory_space=pl.ANY)],
        out_specs=pl.BlockSpec((dec_seq, hd), lambda b, h: (rb + b, h)),
        out_shape=jax.ShapeDtypeStruct(att.shape, att.dtype),
        scratch_shapes=[pltpu.VMEM((dec_seq + past, hd), BF16),
                        pltpu.VMEM((dec_seq + past, hd), BF16)],
        input_output_aliases={10: 0},
        compiler_params=_params("parallel", "parallel"),
        name="diff_attention_sample",
    )(proj, proj, proj, cache_k, cache_v, *rope, lam_p, subln.reshape(1, hd), att)


def _gqa_attend(q, kb, vb):
    s = lax.dot_general(q.astype(BF16), kb, _QK_DIMS, preferred_element_type=F32) * (D_HEAD_DIM ** -0.5)
    return jnp.dot(_softmax(s).astype(BF16), vb, preferred_element_type=F32)


def _gqa_prompt_kernel(q_ref, k_ref, v_ref, g_ref, *refs):
    o_ref, ko_ref, vo_ref = refs[-3:]
    gq = g_ref[0:1]
    gk = g_ref[1:2]
    for g in range(D_KV_HEADS):
        kv_cols = slice(g * D_HEAD_DIM, (g + 1) * D_HEAD_DIM)
        k = _rms(k_ref[:, kv_cols]) * gk
        v = v_ref[:, kv_cols]
        ko_ref[0, 0, g] = k
        vo_ref[0, 0, g] = v
        kb = k.astype(BF16)
        vb = v.astype(BF16)
        for r in range(D_REP):
            head = g * D_REP + r
            cols = slice(head * D_HEAD_DIM, (head + 1) * D_HEAD_DIM)
            q = _rms(q_ref[:, cols]) * gq
            o_ref[:, cols] = _gqa_attend(q, kb, vb).astype(o_ref.dtype)


def _gqa_prompt(proj, qk_g, o, n_odd, kv_prev, *, batch, seq, n_rows, c_width):
    hd = D_HEAD_DIM
    qw = D_HEADS * hd
    kvw = D_KV_HEADS * hd
    q0 = c_width // qw
    k0 = (c_width + qw) // kvw
    kv_shape = jax.ShapeDtypeStruct((batch, n_odd, D_KV_HEADS, seq, hd), F32)
    kv_spec = pl.BlockSpec((1, 1, D_KV_HEADS, seq, hd), lambda b: (b, o, 0, 0, 0))
    extra, extra_specs, aliases = [], [], {}
    if kv_prev is not None:
        extra = list(kv_prev)
        extra_specs = [pl.BlockSpec(memory_space=pl.ANY)] * 2
        aliases = {4: 1, 5: 2}
    return pl.pallas_call(
        _gqa_prompt_kernel,
        grid=(batch,),
        in_specs=[pl.BlockSpec((seq, qw), lambda b: (b, q0)),
                  pl.BlockSpec((seq, kvw), lambda b: (b, k0)),
                  pl.BlockSpec((seq, kvw), lambda b: (b, k0 + 1)),
                  pl.BlockSpec(qk_g.shape, lambda b: (0, 0))] + extra_specs,
        out_specs=[pl.BlockSpec((seq, qw), lambda b: (b, 0)), kv_spec, kv_spec],
        out_shape=[jax.ShapeDtypeStruct((n_rows, qw), BF16), kv_shape, kv_shape],
        input_output_aliases=aliases,
        compiler_params=_params("parallel"),
        name="gqa_prompt",
    )(proj, proj, proj, qk_g, *extra)


def _gqa_sample_kernel(q_ref, k_ref, v_ref, ck_ref, cv_ref, cos_ref, su_ref, sd_ref, g_ref,
                       att_in_ref, o_ref, kf_ref, vf_ref, *, tq):
    del att_in_ref
    t = k_ref.shape[0]
    quarter = D_HEAD_DIM // 4
    gq = g_ref[0:1]
    k = _rms(k_ref[...]) * g_ref[1:2]
    kf_ref[0:t] = _rope(k, cos_ref[...], su_ref[...], sd_ref[...], quarter).astype(BF16)
    kf_ref[t:] = ck_ref[0, 0, 0].astype(BF16)
    vf_ref[0:t] = v_ref[...].astype(BF16)
    vf_ref[t:] = cv_ref[0, 0, 0].astype(BF16)
    for r in range(D_REP):
        cols = slice(r * D_HEAD_DIM, (r + 1) * D_HEAD_DIM)
        for i in range(t // tq):
            rows = slice(i * tq, (i + 1) * tq)
            q = _rms(q_ref[rows, cols]) * gq
            q = _rope(q, cos_ref[rows, :], su_ref[rows, :], sd_ref[rows, :], quarter)
            o_ref[rows, cols] = _gqa_attend(q, kf_ref[...], vf_ref[...]).astype(o_ref.dtype)


def _gqa_sample(proj, cache_k, cache_v, o, rope, qk_g, att, *, dec_batch, dec_seq, n_prompt_rows,
                c_width):
    hd = D_HEAD_DIM
    qw = D_REP * hd
    q0 = c_width // qw
    k0 = (c_width + D_HEADS * hd) // hd
    v0 = k0 + D_KV_HEADS
    past = cache_k.shape[3]
    rb = n_prompt_rows // dec_seq
    cache_spec = pl.BlockSpec((1, 1, 1, past, hd), lambda b, g: (b, o, g, 0, 0))
    tab_spec = pl.BlockSpec((dec_seq, hd), lambda b, g: (0, 0))
    return pl.pallas_call(
        functools.partial(_gqa_sample_kernel, tq=256),
        grid=(dec_batch, D_KV_HEADS),
        in_specs=[pl.BlockSpec((dec_seq, qw), lambda b, g: (rb + b, q0 + g)),
                  pl.BlockSpec((dec_seq, hd), lambda b, g: (rb + b, k0 + g)),
                  pl.BlockSpec((dec_seq, hd), lambda b, g: (rb + b, v0 + g)),
                  cache_spec, cache_spec, tab_spec, tab_spec, tab_spec,
                  pl.BlockSpec(qk_g.shape, lambda b, g: (0, 0)),
                  pl.BlockSpec(memory_space=pl.ANY)],
        out_specs=pl.BlockSpec((dec_seq, qw), lambda b, g: (rb + b, g)),
        out_shape=jax.ShapeDtypeStruct(att.shape, att.dtype),
        scratch_shapes=[pltpu.VMEM((dec_seq + past, hd), BF16),
                        pltpu.VMEM((dec_seq + past, hd), BF16)],
        input_output_aliases={9: 0},
        compiler_params=_params("parallel", "parallel"),
        name="gqa_sample",
    )(proj, proj, proj, cache_k, cache_v, *rope, qk_g, att)


CONV_ROWS = 256
CONV_HALO = 16


def _conv_kernel(a_ref, g_ref, ap_ref, gp_ref, an_ref, gn_ref, w_ref, b_ref, lng_ref, lnb_ref,
                 o_ref, pad_ref, cv_ref, *, n_prompt_blocks, blocks_per_seq):
    rows, width = a_ref.shape
    n_chunks = width // LANES
    i = pl.program_id(0)
    j = jnp.maximum(i - n_prompt_blocks, 0) % blocks_per_seq
    latent = i >= n_prompt_blocks
    has_prev = jnp.where(jnp.logical_and(latent, j > 0), 1.0, 0.0)
    has_next = jnp.where(jnp.logical_and(latent, j < blocks_per_seq - 1), 1.0, 0.0)

    glu = lambda a, g: a * jax.nn.sigmoid(g)
    h_prev = glu(ap_ref[...], gp_ref[...]) * has_prev
    h_cur = glu(a_ref[...], g_ref[...])
    h_next = glu(an_ref[...], gn_ref[...]) * has_next
    for c in range(n_chunks):
        lanes = slice(c * LANES, (c + 1) * LANES)
        pad_ref[c, 0:CONV_HALO, :] = h_prev[:, lanes]
        pad_ref[c, CONV_HALO:CONV_HALO + rows, :] = h_cur[:, lanes]
        pad_ref[c, CONV_HALO + rows:, :] = h_next[:, lanes]

    def chunk_body(c, carry):
        for r in range(rows // SUBLANES):
            base = CONV_HALO - B_PAD + r * SUBLANES
            acc = w_ref[c, 0] * pad_ref[c, pl.ds(base, SUBLANES), :]
            for tap in range(1, B_KERNEL):
                acc = acc + w_ref[c, tap] * pad_ref[c, pl.ds(base + tap, SUBLANES), :]
            cv_ref[c, pl.ds(r * SUBLANES, SUBLANES), :] = acc
        return carry

    lax.fori_loop(0, n_chunks, chunk_body, 0)

    x = jnp.concatenate([cv_ref[c] for c in range(n_chunks)], axis=1) + b_ref[...]
    mu = jnp.mean(x, axis=-1, keepdims=True)
    xc = x - mu
    var = jnp.mean(xc * xc, axis=-1, keepdims=True)
    y = xc * lax.rsqrt(var + EPS) * lng_ref[...] + lnb_ref[...]
    o_ref[...] = (y * jax.nn.sigmoid(y)).astype(o_ref.dtype)


def _conv(proj, conv_w, conv_b, conv_ln, *, col0, n_prompt_rows, prompt_seq, dec_seq):
    m = proj.shape[0]
    taps, width = conv_w.shape
    assert prompt_seq == CONV_ROWS and dec_seq % CONV_ROWS == 0
    n_chunks = width // LANES
    a_blk = col0 // width
    halo_per_blk = CONV_ROWS // CONV_HALO
    last_halo = m // CONV_HALO - 1
    prev_map = lambda cb: (lambda i: (jnp.maximum(i * halo_per_blk - 1, 0), cb))
    next_map = lambda cb: (lambda i: (jnp.minimum((i + 1) * halo_per_blk, last_halo), cb))
    w8 = jnp.broadcast_to(conv_w.reshape(taps, 1, n_chunks, LANES), (taps, SUBLANES, n_chunks, LANES))
    w8 = w8.transpose(2, 0, 1, 3)
    row_spec = lambda: pl.BlockSpec((1, width), lambda i: (0, 0))
    return pl.pallas_call(
        functools.partial(_conv_kernel, n_prompt_blocks=n_prompt_rows // CONV_ROWS,
                          blocks_per_seq=dec_seq // CONV_ROWS),
        grid=(m // CONV_ROWS,),
        in_specs=[pl.BlockSpec((CONV_ROWS, width), lambda i: (i, a_blk)),
                  pl.BlockSpec((CONV_ROWS, width), lambda i: (i, a_blk + 1)),
                  pl.BlockSpec((CONV_HALO, width), prev_map(a_blk)),
                  pl.BlockSpec((CONV_HALO, width), prev_map(a_blk + 1)),
                  pl.BlockSpec((CONV_HALO, width), next_map(a_blk)),
                  pl.BlockSpec((CONV_HALO, width), next_map(a_blk + 1)),
                  pl.BlockSpec(w8.shape, lambda i: (0, 0, 0, 0)),
                  row_spec(), row_spec(), row_spec()],
        out_specs=pl.BlockSpec((CONV_ROWS, width), lambda i: (i, 0)),
        out_shape=jax.ShapeDtypeStruct((m, width), BF16),
        scratch_shapes=[pltpu.VMEM((n_chunks, CONV_ROWS + 2 * CONV_HALO, LANES), F32),
                        pltpu.VMEM((n_chunks, CONV_ROWS, LANES), F32)],
        compiler_params=_params("parallel"),
        name="conformer_conv",
    )(proj, proj, proj, proj, proj, proj, w8, conv_b.reshape(1, width),
      conv_ln[0].reshape(1, width), conv_ln[1].reshape(1, width))


def _s5_lag_kernel(ar_ref, ai_ref, ldt_ref, btr_ref, bti_ref, cr_ref, ci_ref,
                   klag_ref, wtr_ref, wti_ref, vr_ref, vi_ref, alr_ref, ali_ref):
    ar = ar_ref[0]
    ai = ai_ref[0]
    dt = jnp.exp(ldt_ref[0])
    mag = jnp.exp(ar * dt)
    abr = mag * jnp.cos(ai * dt)
    abi = mag * jnp.sin(ai * dt)
    den = ar * ar + ai * ai
    nr = abr - 1.0
    ni = abi
    kr = (nr * ar + ni * ai) / den
    ki = (ni * ar - nr * ai) / den
    al_r, al_i = [], []
    for gi in range(ar.shape[0]):
        row = slice(gi, gi + 1)
        a_r, a_i = abr[row], abi[row]
        bbr = kr[row] * btr_ref[0, gi] - ki[row] * bti_ref[0, gi]
        bbi = kr[row] * bti_ref[0, gi] + ki[row] * btr_ref[0, gi]
        cr = cr_ref[0, gi]
        ci = ci_ref[0, gi]
        pr, pi = jnp.ones_like(a_r), jnp.zeros_like(a_r)
        wt_r, wt_i, v_r, v_i = [], [], [], []
        for lag in range(S5_CHUNK + 1):
            if lag < S5_CHUNK:
                wt_r.append(pr * bbr - pi * bbi)
                wt_i.append(pr * bbi + pi * bbr)
            if lag >= 1:
                v_r.append(cr * pr - ci * pi)
                v_i.append(cr * pi + ci * pr)
            if lag < S5_CHUNK:
                pr, pi = pr * a_r - pi * a_i, pr * a_i + pi * a_r
        al_r.append(pr)
        al_i.append(pi)
        wtr = jnp.concatenate(wt_r, axis=0)
        wti = jnp.concatenate(wt_i, axis=0)
        wtr_ref[0, gi] = wtr
        wti_ref[0, gi] = wti
        vr_ref[0, gi] = jnp.concatenate(v_r, axis=0)
        vi_ref[0, gi] = jnp.concatenate(v_i, axis=0)
        hi = lax.Precision.HIGHEST
        klag_ref[0, gi] = (
            lax.dot_general(cr, wtr, _QK_DIMS, precision=hi, preferred_element_type=F32)
            - lax.dot_general(ci, wti, _QK_DIMS, precision=hi, preferred_element_type=F32))
    alr_ref[0] = jnp.concatenate(al_r, axis=0)
    ali_ref[0] = jnp.concatenate(al_i, axis=0)


def _s5_lag_tables(a_re, a_im, log_dt, b, c):
    nd, g, p = a_re.shape
    h = b.shape[-1]
    gb = S5_LAG_GROUPS_PER_STEP
    rows = S5_CHUNK * h
    bt = b.transpose(0, 1, 2, 4, 3)
    ldt = jnp.broadcast_to(log_dt[:, :, None], (nd, g, p))
    vec = pl.BlockSpec((1, gb, p), lambda d, i: (d, i, 0))
    mat = pl.BlockSpec((1, gb, h, p), lambda d, i: (d, i, 0, 0))
    tall = pl.BlockSpec((1, gb, rows, p), lambda d, i: (d, i, 0, 0))
    tall_shape = jax.ShapeDtypeStruct((nd, g, rows, p), F32)
    vec_shape = jax.ShapeDtypeStruct((nd, g, p), F32)
    return pl.pallas_call(
        _s5_lag_kernel,
        grid=(nd, g // gb),
        in_specs=[vec, vec, vec, mat, mat, mat, mat],
        out_specs=[pl.BlockSpec((1, gb, h, rows), lambda d, i: (d, i, 0, 0)),
                   tall, tall, tall, tall, vec, vec],
        out_shape=[jax.ShapeDtypeStruct((nd, g, h, rows), F32),
                   tall_shape, tall_shape, tall_shape, tall_shape, vec_shape, vec_shape],
        compiler_params=_params("parallel", "parallel"),
        name="s5_lag_tables",
    )(a_re, a_im, ldt, bt[:, 0], bt[:, 1], c[:, 0], c[:, 1])


def _s5_chunk_weights(klag, wtr, wti, vr, vi, alr, ali):
    nd, g, h, _ = klag.shape
    p = wtr.shape[-1]
    n_lag = S5_CHUNK
    k5 = klag.reshape(nd, g, h, n_lag, h)
    s_idx = jnp.arange(n_lag)[:, None]
    t_idx = jnp.arange(n_lag)[None, :]
    kf = jnp.where((t_idx >= s_idx)[None, None, :, :, None],
                   k5[0][:, :, jnp.maximum(t_idx - s_idx, 0), :], 0.0)
    kb = jnp.where((s_idx >= t_idx)[None, None, :, :, None],
                   k5[1][:, :, jnp.maximum(s_idx - t_idx, 0), :], 0.0)
    m_w = (kf + kb).transpose(0, 2, 4, 3, 1).reshape(g, n_lag * h, n_lag * h)
    flip = lambda w: w.reshape(g, n_lag, h, p)[:, ::-1].reshape(g, n_lag * h, p)
    p_w = jnp.concatenate([flip(wtr[0]), wtr[1], flip(wti[0]), wti[1]], axis=-1)
    to_q = lambda v: v.reshape(g, n_lag, h, p).transpose(0, 3, 1, 2).reshape(g, p, n_lag * h)
    flip_q = lambda v: to_q(v.reshape(g, n_lag, h, p)[:, ::-1].reshape(g, n_lag * h, p))
    q_w = jnp.concatenate([to_q(vr[0]), flip_q(vr[1]), -to_q(vi[0]), -flip_q(vi[1])], axis=1)
    a_l = jnp.concatenate([alr[0], alr[1], ali[0], ali[1]], axis=-1).reshape(g, 1, 4 * p)
    return m_w.astype(BF16), p_w.astype(BF16), q_w.astype(BF16), a_l


def _s5_regroup_kernel(*refs):
    x_refs, o_ref = refs[:-1], refs[-1]
    n_chunks = x_refs[0].shape[0] // S5_CHUNK
    groups_per_vreg = LANES // C_GROUP
    for col, x_ref in enumerate(x_refs):
        xs = [x_ref[pl.ds(s, n_chunks, stride=S5_CHUNK), :] for s in range(S5_CHUNK)]
        for j in range(groups_per_vreg):
            cols = slice(j * C_GROUP, (j + 1) * C_GROUP)
            row = jnp.concatenate([x[:, cols] for x in xs], axis=1)
            o_ref[col * groups_per_vreg + j] = row.astype(o_ref.dtype)


def _s5_regroup(proj, *, width, rows_per_step):
    m = proj.shape[0]
    g = width // C_GROUP
    n_cols = width // LANES
    chunks_per_step = rows_per_step // S5_CHUNK
    col_spec = lambda c: pl.BlockSpec((rows_per_step, LANES), lambda i: (i, c))
    return pl.pallas_call(
        _s5_regroup_kernel,
        grid=(m // rows_per_step,),
        in_specs=[col_spec(c) for c in range(n_cols)],
        out_specs=pl.BlockSpec((g, chunks_per_step, S5_TILE), lambda i: (0, i, 0)),
        out_shape=jax.ShapeDtypeStruct((g, m // S5_CHUNK, S5_TILE), BF16),
        compiler_params=_params("parallel"),
        name="s5_regroup",
    )(*([proj] * n_cols))


def _s5_chunk_kernel(u_ref, m_ref, p_ref, q_ref, al_ref, d_ref, h0_ref, y_ref, hf_ref, x_ref,
                     sa_ref, sd_ref, *, p_seqs, p_steps, s_seqs, s_steps):
    half = x_ref.shape[2]
    lane = lax.broadcasted_iota(jnp.int32, (1, half), 1)
    fwd = lane < (half // 2)

    def scan(out_ref, gi, row0, seqs, steps, sr, si, reverse):
        alr = al_ref[gi, :, 0:half]
        ali = al_ref[gi, :, half:]
        for c in (range(steps - 1, -1, -1) if reverse else range(steps)):
            rs = pl.ds(row0 + c, seqs, stride=steps)
            out_ref[0, rs, :] = sr
            out_ref[1, rs, :] = si
            xr = x_ref[0, rs, :]
            xi = x_ref[1, rs, :]
            sr, si = alr * sr - ali * si + xr, alr * si + ali * sr + xi
        return sr, si

    def group(gi, carry):
        u = u_ref[gi]
        x = jnp.dot(u, p_ref[gi], preferred_element_type=F32)
        x_ref[0] = x[:, 0:half]
        x_ref[1] = x[:, half:]
        zero = jnp.zeros((p_seqs, half), F32)
        fa = scan(sa_ref, gi, 0, p_seqs, p_steps, zero, zero, False)
        fd = scan(sd_ref, gi, 0, p_seqs, p_steps, zero, zero, True)
        h0r = h0_ref[gi, :, 0:half]
        h0i = h0_ref[gi, :, half:]
        s_row0 = p_seqs * p_steps
        scan(sa_ref, gi, s_row0, s_seqs, s_steps, h0r, h0i, False)
        scan(sd_ref, gi, s_row0, s_seqs, s_steps, h0r, h0i, True)
        s = jnp.concatenate([jnp.where(fwd, sa_ref[0], sd_ref[0]),
                             jnp.where(fwd, sa_ref[1], sd_ref[1])], axis=1)
        y = (jnp.dot(u, m_ref[gi], preferred_element_type=F32)
             + jnp.dot(s.astype(BF16), q_ref[gi], preferred_element_type=F32))
        y_ref[gi] = jax.nn.gelu(d_ref[gi] * u.astype(F32) + y)
        hf_ref[gi, :, 0:half] = jnp.where(fwd, fa[0], fd[0])
        hf_ref[gi, :, half:] = jnp.where(fwd, fa[1], fd[1])
        return carry

    lax.fori_loop(0, u_ref.shape[0], group, 0)


def _s5_chunks(u_g, m_w, p_w, q_w, a_l, d_l, h0, *, p_seqs, p_steps, s_seqs, s_steps):
    g, rows, tile = u_g.shape
    gb = S5_GROUPS_PER_STEP
    blk = lambda r, c: pl.BlockSpec((gb, r, c), lambda i: (i, 0, 0))
    return pl.pallas_call(
        functools.partial(_s5_chunk_kernel, p_seqs=p_seqs, p_steps=p_steps, s_seqs=s_seqs,
                          s_steps=s_steps),
        grid=(g // gb,),
        in_specs=[blk(rows, tile), blk(tile, tile), blk(tile, tile), blk(tile, tile),
                  blk(1, tile), blk(1, tile), blk(s_seqs, tile)],
        out_specs=[blk(rows, tile), blk(p_seqs, tile)],
        out_shape=[jax.ShapeDtypeStruct((g, rows, tile), F32),
                   jax.ShapeDtypeStruct((g, p_seqs, tile), F32)],
        scratch_shapes=[pltpu.VMEM((2, rows, tile // 2), F32)] * 3,
        compiler_params=_params("parallel"),
        name="s5_chunks",
    )(u_g, m_w, p_w, q_w, a_l, d_l, h0)


def _s5_glu_kernel(yg_ref, w_ref, b_ref, o_ref, y_ref):
    n_rows = yg_ref.shape[1]
    n_cols = y_ref.shape[0]
    groups_per_vreg = LANES // C_GROUP
    for t in range(S5_CHUNK):
        cols = slice(t * C_GROUP, (t + 1) * C_GROUP)
        for col in range(n_cols):
            piece = jnp.concatenate(
                [yg_ref[col * groups_per_vreg + j, :, cols] for j in range(groups_per_vreg)], axis=1)
            y_ref[col, pl.ds(t, n_rows, stride=S5_CHUNK), :] = piece
    y = jnp.concatenate([y_ref[col] for col in range(n_cols)], axis=1)
    z = jnp.dot(y.astype(BF16), w_ref[0], preferred_element_type=F32) + b_ref[...]
    o_ref[...] = (y * jax.nn.sigmoid(z)).astype(o_ref.dtype)


def _s5_glu(y_g, glu_w_stack, o, glu_b, *, tm):
    g, n_chunks, tile = y_g.shape
    width = g * C_GROUP
    m = n_chunks * S5_CHUNK
    return pl.pallas_call(
        _s5_glu_kernel,
        grid=(m // tm,),
        in_specs=[pl.BlockSpec((g, tm // S5_CHUNK, tile), lambda i: (0, i, 0)),
                  pl.BlockSpec((1, width, width), lambda i: (o, 0, 0)),
                  pl.BlockSpec((1, width), lambda i: (0, 0))],
        out_specs=pl.BlockSpec((tm, width), lambda i: (i, 0)),
        out_shape=jax.ShapeDtypeStruct((m, width), BF16),
        scratch_shapes=[pltpu.VMEM((width // LANES, tm, LANES), F32)],
        compiler_params=_params("parallel"),
        name="s5_ungroup_glu",
    )(y_g, glu_w_stack, glu_b.reshape(1, width))


def _s5_mixer(proj, state0, a_re, a_im, log_dt, b, c, d_skip, *, batch, seq, dec_batch, dec_seq):
    nd, g, p = a_re.shape
    h = b.shape[-1]
    lc = S5_CHUNK
    m_w, p_w, q_w, a_l = _s5_chunk_weights(*_s5_lag_tables(a_re, a_im, log_dt, b, c))
    u_g = _s5_regroup(proj, width=g * h, rows_per_step=lc * lc)
    d_l = jnp.tile(d_skip.reshape(g, 1, h), (1, 1, lc))
    h0 = state0.transpose(3, 0, 2, 1, 4).reshape(g, dec_batch, 4 * p)
    y_g, hf = _s5_chunks(u_g, m_w, p_w, q_w, a_l, d_l, h0, p_seqs=batch, p_steps=seq // lc,
                         s_seqs=dec_batch, s_steps=dec_seq // lc)
    state = hf.reshape(g, batch, 2, 2, p).transpose(1, 3, 2, 0, 4)
    return y_g, state


def _final_norm_kernel(x_ref, g_ref, o_ref):
    o_ref[...] = _rms(x_ref[...]) * g_ref[...]


def _final_norm(x, g, *, row0, n_rows, tm):
    d = x.shape[1]
    b0 = row0 // tm
    return pl.pallas_call(
        _final_norm_kernel,
        grid=(n_rows // tm,),
        in_specs=[pl.BlockSpec((tm, d), lambda i: (b0 + i, 0)),
                  pl.BlockSpec((1, d), lambda i: (0, 0))],
        out_specs=pl.BlockSpec((tm, d), lambda i: (i, 0)),
        out_shape=jax.ShapeDtypeStruct((n_rows, d), F32),
        compiler_params=_params("parallel"),
        name="final_rms_norm",
    )(x, g.reshape(1, d))


def kernel(x_prompt, x_sample, c, cache_a_k, cache_a_v, cache_d_k, cache_d_v, state_c_ssm, c_ctx,
           ada_w, ada_b, norm_g, mlp_w1, mlp_w2, even_w_in, even_w_out, diff_lambda, diff_subln,
           conv_w, conv_b, conv_ln, odd_w_in, odd_w_out, ssm_a_re, ssm_a_im, ssm_log_dt, ssm_b,
           ssm_c, ssm_d, ssm_glu_w, ssm_glu_b, qk_norm, final_norm):
    batch, seq, d_model = x_prompt.shape
    dec_batch, dec_seq, _ = x_sample.shape
    depth = ada_w.shape[0]
    n_even = even_w_in.shape[0]
    n_odd = odd_w_in.shape[0]
    n_p = batch * seq
    n_s = dec_batch * dec_seq
    n_rows = n_p + n_s
    rows_kw = dict(n_prompt_rows=n_p, dec_seq=dec_seq)
    a_qk = A_HEADS * 2 * A_DK
    a_width = A_HEADS * A_DV
    c_width = ssm_d.shape[1]
    assert dec_seq % seq == 0 and 1 + dec_batch <= SUBLANES

    x = jnp.concatenate([x_prompt.reshape(n_p, d_model), x_sample.reshape(n_s, d_model)], axis=0)
    cond8 = jnp.concatenate([c_ctx[None, :], c,
                             jnp.zeros((SUBLANES - 1 - dec_batch, d_model), F32)], axis=0)
    mods = _ada(cond8, ada_w, ada_b).reshape(depth, SUBLANES, 1, N_MOD * d_model)

    rope_a = _rope_tables(dec_seq, A_DK, 2)
    rope_d = _rope_tables(dec_seq, D_HEAD_DIM, 1)
    w1_b, w2_b = mlp_w1.astype(BF16), mlp_w2.astype(BF16)
    even_in_b, even_out_b = even_w_in.astype(BF16), even_w_out.astype(BF16)
    odd_in_b, odd_out_b = odd_w_in.astype(BF16), odd_w_out.astype(BF16)
    glu_w_b = ssm_glu_w.astype(BF16)

    a_kv, d_kv, c_st = None, None, []
    for l in range(depth):
        if l % 2 == 0:
            e = l // 2
            lam_init = 0.8 - 0.6 * math.exp(-0.3 * l)
            proj = _norm_mm(x, norm_g[l, 0], mods, l, 0, even_in_b, e, relu2=False,
                            out_dtype=F32, tm=1024, tn=1024, **rows_kw)
            att, a_k, a_v = _dattn_prompt(proj, diff_lambda[e], diff_subln[e], lam_init, e, n_even,
                                          a_kv, batch=batch, seq=seq, n_rows=n_rows)
            a_kv = (a_k, a_v)
            att = _dattn_sample(proj, cache_a_k, cache_a_v, e, rope_a, diff_lambda[e],
                                diff_subln[e], lam_init, att, dec_batch=dec_batch, dec_seq=dec_seq,
                                n_prompt_rows=n_p)
            conv = _conv(proj, conv_w[e], conv_b[e], conv_ln[e], col0=2 * a_qk + a_width,
                         n_prompt_rows=n_p, prompt_seq=seq, dec_seq=dec_seq)
            x = _mm_res([att, conv], even_out_b, e, x, mods, l, 2,
                        tm=1024, tn=1024, tk=a_width, **rows_kw)
        else:
            o = l // 2
            proj = _norm_mm(x, norm_g[l, 0], mods, l, 0, odd_in_b, o, relu2=False,
                            out_dtype=F32, tm=1024, tn=1280, **rows_kw)
            y_g, state = _s5_mixer(proj, state_c_ssm[:, o], ssm_a_re[o], ssm_a_im[o],
                                   ssm_log_dt[o], ssm_b[o], ssm_c[o], ssm_d[o], batch=batch,
                                   seq=seq, dec_batch=dec_batch, dec_seq=dec_seq)
            ssm_out = _s5_glu(y_g, glu_w_b, o, ssm_glu_b[o], tm=1024)
            att, d_k, d_v = _gqa_prompt(proj, qk_norm[o], o, n_odd, d_kv, batch=batch, seq=seq,
                                        n_rows=n_rows, c_width=c_width)
            d_kv = (d_k, d_v)
            att = _gqa_sample(proj, cache_d_k, cache_d_v, o, rope_d, qk_norm[o], att,
                              dec_batch=dec_batch, dec_seq=dec_seq, n_prompt_rows=n_p,
                              c_width=c_width)
            x = _mm_res([ssm_out, att], odd_out_b, o, x, mods, l, 2,
                        tm=1024, tn=1024, tk=c_width, **rows_kw)
            c_st.append(state)
        hidden = _norm_mm(x, norm_g[l, 1], mods, l, 3, w1_b, l, relu2=True,
                          out_dtype=BF16, tm=1024, tn=1024, **rows_kw)
        x = _mm_res([hidden], w2_b, l, x, mods, l, 5, tm=1024, tn=1024, tk=2048, **rows_kw)

    y_prompt = _final_norm(x, final_norm, row0=0, n_rows=n_p, tm=512).reshape(batch, seq, d_model)
    y_sample = _final_norm(x, final_norm, row0=n_p, n_rows=n_s, tm=512)
    y_sample = y_sample.reshape(dec_batch, dec_seq, d_model)
    return (y_prompt, y_sample, a_kv[0], a_kv[1], d_kv[0], d_kv[1], jnp.stack(c_st, axis=1))
```

```python
import functools
import math

import jax
import jax.numpy as jnp
from jax import lax
from jax.experimental import pallas as pl
from jax.experimental.pallas import tpu as pltpu

F32 = jnp.float32
BF16 = jnp.bfloat16

EPS = 1e-6
ROPE_BASE = 10000.0
GRID_W = 64
N_MOD = 6
A_HEADS = 8
A_DK = 64
A_DV = 2 * A_DK
B_KERNEL = 31
B_PAD = (B_KERNEL - 1) // 2
C_GROUP = 16
C_STATE = 64
D_HEADS = 8
D_KV_HEADS = 2
D_REP = D_HEADS // D_KV_HEADS
D_HEAD_DIM = 128

LANES = 128
SUBLANES = 8
VMEM_LIMIT = 56 * 1024 * 1024
NORM_ROWS = 16

S5_CHUNK = 16
S5_TILE = S5_CHUNK * C_GROUP
S5_GROUPS_PER_STEP = 4
S5_LAG_GROUPS_PER_STEP = 8


def _params(*sem):
    return pltpu.CompilerParams(dimension_semantics=sem, vmem_limit_bytes=VMEM_LIMIT)


def _mod_row(row0, n_prompt_rows, dec_seq):
    return jnp.maximum(row0 + (dec_seq - n_prompt_rows), 0) // dec_seq


def _ada_kernel(c_ref, w_ref, b_ref, o_ref):
    c = c_ref[...]
    s = (c * jax.nn.sigmoid(c)).astype(BF16)
    o_ref[0] = jnp.dot(s, w_ref[0].astype(BF16), preferred_element_type=F32) + b_ref[0]


def _ada(cond8, ada_w, ada_b):
    depth, d, n = ada_w.shape
    tn = 1024
    return pl.pallas_call(
        _ada_kernel,
        grid=(depth, n // tn),
        in_specs=[pl.BlockSpec((SUBLANES, d), lambda l, j: (0, 0)),
                  pl.BlockSpec((1, d, tn), lambda l, j: (l, 0, j)),
                  pl.BlockSpec((1, 1, tn), lambda l, j: (l, 0, j))],
        out_specs=pl.BlockSpec((1, SUBLANES, tn), lambda l, j: (l, 0, j)),
        out_shape=jax.ShapeDtypeStruct((depth, SUBLANES, n), F32),
        compiler_params=_params("parallel", "parallel"),
        name="ada_modulation",
    )(cond8, ada_w, ada_b.reshape(depth, 1, n))


def _rms(x):
    return x * lax.rsqrt(jnp.mean(x * x, axis=-1, keepdims=True) + EPS)


def _norm_mm_kernel(x_ref, g_ref, sh_ref, sc_ref, w_ref, o_ref, h_ref, *, relu2):
    @pl.when(pl.program_id(1) == 0)
    def _():
        gain = g_ref[...] * (1.0 + sc_ref[0, 0])
        shift = sh_ref[0, 0]

        def rows_body(r, carry):
            rows = pl.ds(pl.multiple_of(r * NORM_ROWS, NORM_ROWS), NORM_ROWS)
            h_ref[rows, :] = (_rms(x_ref[rows, :]) * gain + shift).astype(BF16)
            return carry

        lax.fori_loop(0, x_ref.shape[0] // NORM_ROWS, rows_body, 0, unroll=4)

    acc = jnp.dot(h_ref[...], w_ref[0], preferred_element_type=F32)
    if relu2:
        acc = jnp.maximum(acc, 0.0)
        acc = acc * acc
    o_ref[...] = acc.astype(o_ref.dtype)


def _norm_mm(x, g, mods, layer, chunk, w_stack, w_layer, *, relu2, out_dtype, tm, tn,
             n_prompt_rows, dec_seq):
    m, d = x.shape
    n = w_stack.shape[2]
    row = lambda i: _mod_row(i * tm, n_prompt_rows, dec_seq)
    return pl.pallas_call(
        functools.partial(_norm_mm_kernel, relu2=relu2),
        grid=(m // tm, n // tn),
        in_specs=[pl.BlockSpec((tm, d), lambda i, j: (i, 0)),
                  pl.BlockSpec((1, d), lambda i, j: (0, 0)),
                  pl.BlockSpec((1, 1, 1, d), lambda i, j: (layer, row(i), 0, chunk)),
                  pl.BlockSpec((1, 1, 1, d), lambda i, j: (layer, row(i), 0, chunk + 1)),
                  pl.BlockSpec((1, d, tn), lambda i, j: (w_layer, 0, j))],
        out_specs=pl.BlockSpec((tm, tn), lambda i, j: (i, j)),
        out_shape=jax.ShapeDtypeStruct((m, n), out_dtype),
        scratch_shapes=[pltpu.VMEM((tm, d), BF16)],
        compiler_params=_params("parallel", "arbitrary"),
        name="norm_mod_matmul",
    )(x, g.reshape(1, d), mods, mods, w_stack)


def _mm_res_kernel(*refs, n_parts, nk):
    a_refs = refs[:n_parts]
    w_refs = refs[n_parts:2 * n_parts]
    res_ref, gate_ref, o_ref = refs[2 * n_parts:2 * n_parts + 3]
    part = jnp.dot(a_refs[0][...], w_refs[0][0], preferred_element_type=F32)
    for a_ref, w_ref in zip(a_refs[1:], w_refs[1:]):
        part = part + jnp.dot(a_ref[...], w_ref[0], preferred_element_type=F32)
    if nk == 1:
        o_ref[...] = res_ref[...] + gate_ref[0, 0] * part
        return
    acc_ref = refs[-1]
    k = pl.program_id(2)

    @pl.when(k == 0)
    def _():
        acc_ref[...] = part

    @pl.when(k > 0)
    def _():
        acc_ref[...] += part

    @pl.when(k == nk - 1)
    def _():
        o_ref[...] = res_ref[...] + gate_ref[0, 0] * acc_ref[...]


def _mm_res(a_parts, w_stack, w_layer, res, mods, layer, chunk, *, tm, tn, tk, n_prompt_rows,
            dec_seq):
    m, n = res.shape
    n_parts = len(a_parts)
    kdim = a_parts[0].shape[1]
    nk = kdim // tk
    gate_blk = chunk * (n // tn)
    row = lambda i: _mod_row(i * tm, n_prompt_rows, dec_seq)
    w_spec = lambda p: pl.BlockSpec((1, tk, tn), lambda i, j, k: (w_layer, p * nk + k, j))
    in_specs = ([pl.BlockSpec((tm, tk), lambda i, j, k: (i, k)) for _ in a_parts]
                + [w_spec(p) for p in range(n_parts)]
                + [pl.BlockSpec((tm, tn), lambda i, j, k: (i, j)),
                   pl.BlockSpec((1, 1, 1, tn), lambda i, j, k: (layer, row(i), 0, gate_blk + j))])
    scratch = [pltpu.VMEM((tm, tn), F32)] if nk > 1 else []
    return pl.pallas_call(
        functools.partial(_mm_res_kernel, n_parts=n_parts, nk=nk),
        grid=(m // tm, n // tn, nk),
        in_specs=in_specs,
        out_specs=pl.BlockSpec((tm, tn), lambda i, j, k: (i, j)),
        out_shape=jax.ShapeDtypeStruct((m, n), F32),
        scratch_shapes=scratch,
        compiler_params=_params("parallel", "parallel", "arbitrary"),
        name="matmul_gated_residual",
    )(*a_parts, *([w_stack] * n_parts), res, mods)


def _softmax(s):
    e = jnp.exp(s - jnp.max(s, axis=-1, keepdims=True))
    return e / jnp.sum(e, axis=-1, keepdims=True)


_QK_DIMS = (((1,), (1,)), ((), ()))


def _rope(x, cos, sin_up, sin_dn, quarter):
    width = x.shape[-1]
    return (x * cos + pltpu.roll(x, width - quarter, axis=1) * sin_up
            + pltpu.roll(x, quarter, axis=1) * sin_dn)


def _rope_tables(n_tokens, dim, reps):
    rows = n_tokens // GRID_W
    row = jnp.repeat(jnp.arange(rows, dtype=F32), GRID_W)
    col = jnp.tile(jnp.arange(GRID_W, dtype=F32), rows)
    quarter = dim // 4
    inv_freq = ROPE_BASE ** (-jnp.arange(quarter, dtype=F32) / quarter)
    ang_r = row[:, None] * inv_freq[None, :]
    ang_c = col[:, None] * inv_freq[None, :]
    ang = jnp.concatenate([ang_r, ang_r, ang_c, ang_c], axis=-1)
    cos, sin = jnp.cos(ang), jnp.sin(ang)
    even_chunk = ((jnp.arange(dim) // quarter) % 2 == 0)[None, :]
    sin_up = jnp.where(even_chunk, -sin, 0.0)
    sin_dn = jnp.where(even_chunk, 0.0, sin)
    tile = lambda t: jnp.tile(t, (1, reps))
    return tile(cos), tile(sin_up), tile(sin_dn)


def _diff_lambda(lam_ref, lam_init):
    lp = lam_ref[...]
    a = jnp.sum(lp[0:1] * lp[1:2], axis=-1, keepdims=True)
    b = jnp.sum(lp[2:3] * lp[3:4], axis=-1, keepdims=True)
    return jnp.exp(a) - jnp.exp(b) + lam_init


def _diff_attend(q, kb, vb, lam, subln, lam_init):
    lane = lax.broadcasted_iota(jnp.int32, q.shape, 1)
    q1 = jnp.where(lane < A_DK, q, 0.0).astype(BF16)
    q2 = jnp.where(lane >= A_DK, q, 0.0).astype(BF16)
    scale = A_DK ** -0.5
    s1 = lax.dot_general(q1, kb, _QK_DIMS, preferred_element_type=F32) * scale
    s2 = lax.dot_general(q2, kb, _QK_DIMS, preferred_element_type=F32) * scale
    p = _softmax(s1) - lam * _softmax(s2)
    o = jnp.dot(p.astype(BF16), vb, preferred_element_type=F32)
    return _rms(o) * subln * (1.0 - lam_init)


def _dattn_prompt_kernel(q_ref, k_ref, v_ref, lam_ref, sub_ref, *refs, lam_init):
    o_ref, ko_ref, vo_ref = refs[-3:]
    lam = _diff_lambda(lam_ref, lam_init)
    for h in range(A_HEADS):
        cols = slice(h * A_DV, (h + 1) * A_DV)
        k = k_ref[:, cols]
        v = v_ref[:, cols]
        ko_ref[0, 0, h] = k
        vo_ref[0, 0, h] = v
        o = _diff_attend(q_ref[:, cols], k.astype(BF16), v.astype(BF16), lam, sub_ref[...],
                         lam_init)
        o_ref[:, cols] = o.astype(o_ref.dtype)


def _dattn_prompt(proj, lam_p, subln, lam_init, e, n_even, kv_prev, *, batch, seq, n_rows):
    hd = 2 * A_DK
    width = A_HEADS * hd
    kv_shape = jax.ShapeDtypeStruct((batch, n_even, A_HEADS, seq, hd), F32)
    kv_spec = pl.BlockSpec((1, 1, A_HEADS, seq, hd), lambda b: (b, e, 0, 0, 0))
    extra, extra_specs, aliases = [], [], {}
    if kv_prev is not None:
        extra = list(kv_prev)
        extra_specs = [pl.BlockSpec(memory_space=pl.ANY)] * 2
        aliases = {5: 1, 6: 2}
    return pl.pallas_call(
        functools.partial(_dattn_prompt_kernel, lam_init=lam_init),
        grid=(batch,),
        in_specs=[pl.BlockSpec((seq, width), lambda b: (b, 0)),
                  pl.BlockSpec((seq, width), lambda b: (b, 1)),
                  pl.BlockSpec((seq, width), lambda b: (b, 2)),
                  pl.BlockSpec(lam_p.shape, lambda b: (0, 0)),
                  pl.BlockSpec((1, hd), lambda b: (0, 0))] + extra_specs,
        out_specs=[pl.BlockSpec((seq, width), lambda b: (b, 0)), kv_spec, kv_spec],
        out_shape=[jax.ShapeDtypeStruct((n_rows, width), BF16), kv_shape, kv_shape],
        input_output_aliases=aliases,
        compiler_params=_params("parallel"),
        name="diff_attention_prompt",
    )(proj, proj, proj, lam_p, subln.reshape(1, hd), *extra)


def _dattn_sample_kernel(q_ref, k_ref, v_ref, ck_ref, cv_ref, cos_ref, su_ref, sd_ref, lam_ref,
                         sub_ref, att_in_ref, o_ref, kf_ref, vf_ref, *, lam_init, tq):
    del att_in_ref
    t = k_ref.shape[0]
    quarter = A_DK // 4
    kf_ref[0:t] = _rope(k_ref[...], cos_ref[...], su_ref[...], sd_ref[...], quarter).astype(BF16)
    kf_ref[t:] = ck_ref[0, 0, 0].astype(BF16)
    vf_ref[0:t] = v_ref[...].astype(BF16)
    vf_ref[t:] = cv_ref[0, 0, 0].astype(BF16)
    lam = _diff_lambda(lam_ref, lam_init)
    for i in range(t // tq):
        rows = slice(i * tq, (i + 1) * tq)
        q = _rope(q_ref[rows, :], cos_ref[rows, :], su_ref[rows, :], sd_ref[rows, :], quarter)
        o = _diff_attend(q, kf_ref[...], vf_ref[...], lam, sub_ref[...], lam_init)
        o_ref[rows, :] = o.astype(o_ref.dtype)


def _dattn_sample(proj, cache_k, cache_v, e, rope, lam_p, subln, lam_init, att, *,
                  dec_batch, dec_seq, n_prompt_rows):
    hd = 2 * A_DK
    past = cache_k.shape[3]
    rb = n_prompt_rows // dec_seq
    cache_spec = pl.BlockSpec((1, 1, 1, past, hd), lambda b, h: (b, e, h, 0, 0))
    tab_spec = pl.BlockSpec((dec_seq, hd), lambda b, h: (0, 0))
    return pl.pallas_call(
        functools.partial(_dattn_sample_kernel, lam_init=lam_init, tq=256),
        grid=(dec_batch, A_HEADS),
        in_specs=[pl.BlockSpec((dec_seq, hd), lambda b, h: (rb + b, h)),
                  pl.BlockSpec((dec_seq, hd), lambda b, h: (rb + b, A_HEADS + h)),
                  pl.BlockSpec((dec_seq, hd), lambda b, h: (rb + b, 2 * A_HEADS + h)),
                  cache_spec, cache_spec, tab_spec, tab_spec, tab_spec,
                  pl.BlockSpec(lam_p.shape, lambda b, h: (0, 0)),
                  pl.BlockSpec((1, hd), lambda b, h: (0, 0)),
                  pl.BlockSpec(memory_space=pl.ANY)],
        out_specs=pl.BlockSpec((dec_seq, hd), lambda b, h: (rb + b, h)),
        out_shape=jax.ShapeDtypeStruct(att.shape, att.dtype),
        scratch_shapes=[pltpu.VMEM((dec_seq + past, hd), BF16),
                        pltpu.VMEM((dec_seq + past, hd), BF16)],
        input_output_aliases={10: 0},
        compiler_params=_params("parallel", "parallel"),
        name="diff_attention_sample",
    )(proj, proj, proj, cache_k, cache_v, *rope, lam_p, subln.reshape(1, hd), att)


def _gqa_attend(q, kb, vb):
    s = lax.dot_general(q.astype(BF16), kb, _QK_DIMS, preferred_element_type=F32) * (D_HEAD_DIM ** -0.5)
    return jnp.dot(_softmax(s).astype(BF16), vb, preferred_element_type=F32)


def _gqa_prompt_kernel(q_ref, k_ref, v_ref, g_ref, *refs):
    o_ref, ko_ref, vo_ref = refs[-3:]
    gq = g_ref[0:1]
    gk = g_ref[1:2]
    for g in range(D_KV_HEADS):
        kv_cols = slice(g * D_HEAD_DIM, (g + 1) * D_HEAD_DIM)
        k = _rms(k_ref[:, kv_cols]) * gk
        v = v_ref[:, kv_cols]
        ko_ref[0, 0, g] = k
        vo_ref[0, 0, g] = v
        kb = k.astype(BF16)
        vb = v.astype(BF16)
        for r in range(D_REP):
            head = g * D_REP + r
            cols = slice(head * D_HEAD_DIM, (head + 1) * D_HEAD_DIM)
            q = _rms(q_ref[:, cols]) * gq
            o_ref[:, cols] = _gqa_attend(q, kb, vb).astype(o_ref.dtype)


def _gqa_prompt(proj, qk_g, o, n_odd, kv_prev, *, batch, seq, n_rows, c_width):
    hd = D_HEAD_DIM
    qw = D_HEADS * hd
    kvw = D_KV_HEADS * hd
    q0 = c_width // qw
    k0 = (c_width + qw) // kvw
    kv_shape = jax.ShapeDtypeStruct((batch, n_odd, D_KV_HEADS, seq, hd), F32)
    kv_spec = pl.BlockSpec((1, 1, D_KV_HEADS, seq, hd), lambda b: (b, o, 0, 0, 0))
    extra, extra_specs, aliases = [], [], {}
    if kv_prev is not None:
        extra = list(kv_prev)
        extra_specs = [pl.BlockSpec(memory_space=pl.ANY)] * 2
        aliases = {4: 1, 5: 2}
    return pl.pallas_call(
        _gqa_prompt_kernel,
        grid=(batch,),
        in_specs=[pl.BlockSpec((seq, qw), lambda b: (b, q0)),
                  pl.BlockSpec((seq, kvw), lambda b: (b, k0)),
                  pl.BlockSpec((seq, kvw), lambda b: (b, k0 + 1)),
                  pl.BlockSpec(qk_g.shape, lambda b: (0, 0))] + extra_specs,
        out_specs=[pl.BlockSpec((seq, qw), lambda b: (b, 0)), kv_spec, kv_spec],
        out_shape=[jax.ShapeDtypeStruct((n_rows, qw), BF16), kv_shape, kv_shape],
        input_output_aliases=aliases,
        compiler_params=_params("parallel"),
        name="gqa_prompt",
    )(proj, proj, proj, qk_g, *extra)


def _gqa_sample_kernel(q_ref, k_ref, v_ref, ck_ref, cv_ref, cos_ref, su_ref, sd_ref, g_ref,
                       att_in_ref, o_ref, kf_ref, vf_ref, *, tq):
    del att_in_ref
    t = k_ref.shape[0]
    quarter = D_HEAD_DIM // 4
    gq = g_ref[0:1]
    k = _rms(k_ref[...]) * g_ref[1:2]
    kf_ref[0:t] = _rope(k, cos_ref[...], su_ref[...], sd_ref[...], quarter).astype(BF16)
    kf_ref[t:] = ck_ref[0, 0, 0].astype(BF16)
    vf_ref[0:t] = v_ref[...].astype(BF16)
    vf_ref[t:] = cv_ref[0, 0, 0].astype(BF16)
    for r in range(D_REP):
        cols = slice(r * D_HEAD_DIM, (r + 1) * D_HEAD_DIM)
        for i in range(t // tq):
            rows = slice(i * tq, (i + 1) * tq)
            q = _rms(q_ref[rows, cols]) * gq
            q = _rope(q, cos_ref[rows, :], su_ref[rows, :], sd_ref[rows, :], quarter)
            o_ref[rows, cols] = _gqa_attend(q, kf_ref[...], vf_ref[...]).astype(o_ref.dtype)


def _gqa_sample(proj, cache_k, cache_v, o, rope, qk_g, att, *, dec_batch, dec_seq, n_prompt_rows,
                c_width):
    hd = D_HEAD_DIM
    qw = D_REP * hd
    q0 = c_width // qw
    k0 = (c_width + D_HEADS * hd) // hd
    v0 = k0 + D_KV_HEADS
    past = cache_k.shape[3]
    rb = n_prompt_rows // dec_seq
    cache_spec = pl.BlockSpec((1, 1, 1, past, hd), lambda b, g: (b, o, g, 0, 0))
    tab_spec = pl.BlockSpec((dec_seq, hd), lambda b, g: (0, 0))
    return pl.pallas_call(
        functools.partial(_gqa_sample_kernel, tq=256),
        grid=(dec_batch, D_KV_HEADS),
        in_specs=[pl.BlockSpec((dec_seq, qw), lambda b, g: (rb + b, q0 + g)),
                  pl.BlockSpec((dec_seq, hd), lambda b, g: (rb + b, k0 + g)),
                  pl.BlockSpec((dec_seq, hd), lambda b, g: (rb + b, v0 + g)),
                  cache_spec, cache_spec, tab_spec, tab_spec, tab_spec,
                  pl.BlockSpec(qk_g.shape, lambda b, g: (0, 0)),
                  pl.BlockSpec(memory_space=pl.ANY)],
        out_specs=pl.BlockSpec((dec_seq, qw), lambda b, g: (rb + b, g)),
        out_shape=jax.ShapeDtypeStruct(att.shape, att.dtype),
        scratch_shapes=[pltpu.VMEM((dec_seq + past, hd), BF16),
                        pltpu.VMEM((dec_seq + past, hd), BF16)],
        input_output_aliases={9: 0},
        compiler_params=_params("parallel", "parallel"),
        name="gqa_sample",
    )(proj, proj, proj, cache_k, cache_v, *rope, qk_g, att)


CONV_ROWS = 256
CONV_HALO = 16


def _conv_kernel(a_ref, g_ref, ap_ref, gp_ref, an_ref, gn_ref, w_ref, b_ref, lng_ref, lnb_ref,
                 o_ref, pad_ref, cv_ref, *, n_prompt_blocks, blocks_per_seq):
    rows, width = a_ref.shape
    n_chunks = width // LANES
    i = pl.program_id(0)
    j = jnp.maximum(i - n_prompt_blocks, 0) % blocks_per_seq
    latent = i >= n_prompt_blocks
    has_prev = jnp.where(jnp.logical_and(latent, j > 0), 1.0, 0.0)
    has_next = jnp.where(jnp.logical_and(latent, j < blocks_per_seq - 1), 1.0, 0.0)

    glu = lambda a, g: a * jax.nn.sigmoid(g)
    h_prev = glu(ap_ref[...], gp_ref[...]) * has_prev
    h_cur = glu(a_ref[...], g_ref[...])
    h_next = glu(an_ref[...], gn_ref[...]) * has_next
    for c in range(n_chunks):
        lanes = slice(c * LANES, (c + 1) * LANES)
        pad_ref[c, 0:CONV_HALO, :] = h_prev[:, lanes]
        pad_ref[c, CONV_HALO:CONV_HALO + rows, :] = h_cur[:, lanes]
        pad_ref[c, CONV_HALO + rows:, :] = h_next[:, lanes]

    def chunk_body(c, carry):
        for r in range(rows // SUBLANES):
            base = CONV_HALO - B_PAD + r * SUBLANES
            acc = w_ref[c, 0] * pad_ref[c, pl.ds(base, SUBLANES), :]
            for tap in range(1, B_KERNEL):
                acc = acc + w_ref[c, tap] * pad_ref[c, pl.ds(base + tap, SUBLANES), :]
            cv_ref[c, pl.ds(r * SUBLANES, SUBLANES), :] = acc
        return carry

    lax.fori_loop(0, n_chunks, chunk_body, 0)

    x = jnp.concatenate([cv_ref[c] for c in range(n_chunks)], axis=1) + b_ref[...]
    mu = jnp.mean(x, axis=-1, keepdims=True)
    xc = x - mu
    var = jnp.mean(xc * xc, axis=-1, keepdims=True)
    y = xc * lax.rsqrt(var + EPS) * lng_ref[...] + lnb_ref[...]
    o_ref[...] = (y * jax.nn.sigmoid(y)).astype(o_ref.dtype)


def _conv(proj, conv_w, conv_b, conv_ln, *, col0, n_prompt_rows, prompt_seq, dec_seq):
    m = proj.shape[0]
    taps, width = conv_w.shape
    assert prompt_seq == CONV_ROWS and dec_seq % CONV_ROWS == 0
    n_chunks = width // LANES
    a_blk = col0 // width
    halo_per_blk = CONV_ROWS // CONV_HALO
    last_halo = m // CONV_HALO - 1
    prev_map = lambda cb: (lambda i: (jnp.maximum(i * halo_per_blk - 1, 0), cb))
    next_map = lambda cb: (lambda i: (jnp.minimum((i + 1) * halo_per_blk, last_halo), cb))
    w8 = jnp.broadcast_to(conv_w.reshape(taps, 1, n_chunks, LANES), (taps, SUBLANES, n_chunks, LANES))
    w8 = w8.transpose(2, 0, 1, 3)
    row_spec = lambda: pl.BlockSpec((1, width), lambda i: (0, 0))
    return pl.pallas_call(
        functools.partial(_conv_kernel, n_prompt_blocks=n_prompt_rows // CONV_ROWS,
                          blocks_per_seq=dec_seq // CONV_ROWS),
        grid=(m // CONV_ROWS,),
        in_specs=[pl.BlockSpec((CONV_ROWS, width), lambda i: (i, a_blk)),
                  pl.BlockSpec((CONV_ROWS, width), lambda i: (i, a_blk + 1)),
                  pl.BlockSpec((CONV_HALO, width), prev_map(a_blk)),
                  pl.BlockSpec((CONV_HALO, width), prev_map(a_blk + 1)),
                  pl.BlockSpec((CONV_HALO, width), next_map(a_blk)),
                  pl.BlockSpec((CONV_HALO, width), next_map(a_blk + 1)),
                  pl.BlockSpec(w8.shape, lambda i: (0, 0, 0, 0)),
                  row_spec(), row_spec(), row_spec()],
        out_specs=pl.BlockSpec((CONV_ROWS, width), lambda i: (i, 0)),
        out_shape=jax.ShapeDtypeStruct((m, width), BF16),
        scratch_shapes=[pltpu.VMEM((n_chunks, CONV_ROWS + 2 * CONV_HALO, LANES), F32),
                        pltpu.VMEM((n_chunks, CONV_ROWS, LANES), F32)],
        compiler_params=_params("parallel"),
        name="conformer_conv",
    )(proj, proj, proj, proj, proj, proj, w8, conv_b.reshape(1, width),
      conv_ln[0].reshape(1, width), conv_ln[1].reshape(1, width))


def _s5_weights_kernel(ar_ref, ai_ref, ldt_ref, btr_ref, bti_ref, cr_ref, ci_ref,
                       mt_ref, p_ref, qt_ref, al_ref):
    n_lag = S5_CHUNK
    hi = lax.Precision.HIGHEST
    disc = []
    for d in range(2):
        ar = ar_ref[d]
        ai = ai_ref[d]
        dt = jnp.exp(ldt_ref[d])
        mag = jnp.exp(ar * dt)
        abr = mag * jnp.cos(ai * dt)
        abi = mag * jnp.sin(ai * dt)
        den = ar * ar + ai * ai
        nr = abr - 1.0
        ni = abi
        disc.append((abr, abi, (nr * ar + ni * ai) / den, (ni * ar - nr * ai) / den))
    for gi in range(ar_ref.shape[1]):
        row = slice(gi, gi + 1)
        tables = []
        for d in range(2):
            abr, abi, kr, ki = disc[d]
            a_r, a_i = abr[row], abi[row]
            bbr = kr[row] * btr_ref[d, gi] - ki[row] * bti_ref[d, gi]
            bbi = kr[row] * bti_ref[d, gi] + ki[row] * btr_ref[d, gi]
            cr = cr_ref[d, gi]
            ci = ci_ref[d, gi]
            pr, pi = jnp.ones_like(a_r), jnp.zeros_like(a_r)
            wt_r, wt_i, v_r, v_i = [], [], [], []
            for lag in range(n_lag + 1):
                if lag < n_lag:
                    wt_r.append(pr * bbr - pi * bbi)
                    wt_i.append(pr * bbi + pi * bbr)
                if lag >= 1:
                    v_r.append(cr * pr - ci * pi)
                    v_i.append(cr * pi + ci * pr)
                if lag < n_lag:
                    pr, pi = pr * a_r - pi * a_i, pr * a_i + pi * a_r
            if d == 0:
                wt_r.reverse()
                wt_i.reverse()
            else:
                v_r.reverse()
                v_i.reverse()
            wtr = jnp.concatenate(wt_r, axis=0)
            wti = jnp.concatenate(wt_i, axis=0)
            klag = (lax.dot_general(cr, wtr, _QK_DIMS, precision=hi, preferred_element_type=F32)
                    - lax.dot_general(ci, wti, _QK_DIMS, precision=hi, preferred_element_type=F32))
            tables.append((wtr, wti, jnp.concatenate(v_r, axis=0), jnp.concatenate(v_i, axis=0),
                           klag, pr, pi))
        (wtr_f, wti_f, vr_f, vi_f, k_f, alr_f, ali_f), (wtr_b, wti_b, vr_b, vi_b, k_b, alr_b, ali_b) = tables
        p_ref[gi] = jnp.concatenate([wtr_f, wtr_b, wti_f, wti_b], axis=1).astype(p_ref.dtype)
        qt_ref[gi] = jnp.concatenate([vr_f, vr_b, -vi_f, -vi_b], axis=1).astype(qt_ref.dtype)
        al_ref[gi] = jnp.concatenate([alr_f, alr_b, ali_f, ali_b], axis=1)
        h = k_f.shape[0]
        blocks = []
        for t in range(n_lag):
            lo = (n_lag - 1 - t) * h
            fwd = k_f if lo == 0 else jnp.concatenate([k_f[:, lo:], jnp.zeros((h, lo), F32)], axis=1)
            bwd = k_b if t == 0 else jnp.concatenate(
                [jnp.zeros((h, t * h), F32), k_b[:, :(n_lag - t) * h]], axis=1)
            blocks.append(fwd + bwd)
        mt_ref[gi] = jnp.concatenate(blocks, axis=0).astype(mt_ref.dtype)


def _s5_chunk_weights(a_re, a_im, log_dt, b, c):
    nd, g, p = a_re.shape
    h = b.shape[-1]
    gb = S5_LAG_GROUPS_PER_STEP
    tile = S5_CHUNK * h
    bt = b.transpose(0, 1, 2, 4, 3)
    ldt = jnp.broadcast_to(log_dt[:, :, None], (nd, g, p))
    vec = pl.BlockSpec((nd, gb, p), lambda i: (0, i, 0))
    mat = pl.BlockSpec((nd, gb, h, p), lambda i: (0, i, 0, 0))
    sq = pl.BlockSpec((gb, tile, tile), lambda i: (i, 0, 0))
    sq_shape = jax.ShapeDtypeStruct((g, tile, tile), BF16)
    return pl.pallas_call(
        _s5_weights_kernel,
        grid=(g // gb,),
        in_specs=[vec, vec, vec, mat, mat, mat, mat],
        out_specs=[sq, sq, sq, pl.BlockSpec((gb, 1, 4 * p), lambda i: (i, 0, 0))],
        out_shape=[sq_shape, sq_shape, sq_shape, jax.ShapeDtypeStruct((g, 1, 4 * p), F32)],
        compiler_params=_params("parallel"),
        name="s5_chunk_weights",
    )(a_re, a_im, ldt, bt[:, 0], bt[:, 1], c[:, 0], c[:, 1])


def _s5_regroup_kernel(*refs):
    x_refs, o_ref = refs[:-1], refs[-1]
    n_chunks = x_refs[0].shape[0] // S5_CHUNK
    groups_per_vreg = LANES // C_GROUP
    for col, x_ref in enumerate(x_refs):
        xs = [x_ref[pl.ds(s, n_chunks, stride=S5_CHUNK), :] for s in range(S5_CHUNK)]
        for j in range(groups_per_vreg):
            cols = slice(j * C_GROUP, (j + 1) * C_GROUP)
            row = jnp.concatenate([x[:, cols] for x in xs], axis=1)
            o_ref[col * groups_per_vreg + j] = row.astype(o_ref.dtype)


def _s5_regroup(proj, *, width, rows_per_step):
    m = proj.shape[0]
    g = width // C_GROUP
    n_cols = width // LANES
    chunks_per_step = rows_per_step // S5_CHUNK
    col_spec = lambda c: pl.BlockSpec((rows_per_step, LANES), lambda i: (i, c))
    return pl.pallas_call(
        _s5_regroup_kernel,
        grid=(m // rows_per_step,),
        in_specs=[col_spec(c) for c in range(n_cols)],
        out_specs=pl.BlockSpec((g, chunks_per_step, S5_TILE), lambda i: (0, i, 0)),
        out_shape=jax.ShapeDtypeStruct((g, m // S5_CHUNK, S5_TILE), BF16),
        compiler_params=_params("parallel"),
        name="s5_regroup",
    )(*([proj] * n_cols))


def _s5_chunk_kernel(u_ref, m_ref, p_ref, q_ref, al_ref, d_ref, h0_ref, y_ref, hf_ref, x_ref,
                     sa_ref, sd_ref, *, p_seqs, p_steps, s_seqs, s_steps):
    half = x_ref.shape[2]
    lane = lax.broadcasted_iota(jnp.int32, (1, half), 1)
    fwd = lane < (half // 2)

    def scan(out_ref, gi, row0, seqs, steps, sr, si, reverse):
        alr = al_ref[gi, :, 0:half]
        ali = al_ref[gi, :, half:]
        for c in (range(steps - 1, -1, -1) if reverse else range(steps)):
            rs = pl.ds(row0 + c, seqs, stride=steps)
            out_ref[0, rs, :] = sr
            out_ref[1, rs, :] = si
            xr = x_ref[0, rs, :]
            xi = x_ref[1, rs, :]
            sr, si = alr * sr - ali * si + xr, alr * si + ali * sr + xi
        return sr, si

    def group(gi, carry):
        u = u_ref[gi]
        x = jnp.dot(u, p_ref[gi], preferred_element_type=F32)
        x_ref[0] = x[:, 0:half]
        x_ref[1] = x[:, half:]
        zero = jnp.zeros((p_seqs, half), F32)
        fa = scan(sa_ref, gi, 0, p_seqs, p_steps, zero, zero, False)
        fd = scan(sd_ref, gi, 0, p_seqs, p_steps, zero, zero, True)
        h0r = h0_ref[gi, :, 0:half]
        h0i = h0_ref[gi, :, half:]
        s_row0 = p_seqs * p_steps
        scan(sa_ref, gi, s_row0, s_seqs, s_steps, h0r, h0i, False)
        scan(sd_ref, gi, s_row0, s_seqs, s_steps, h0r, h0i, True)
        s = jnp.concatenate([jnp.where(fwd, sa_ref[0], sd_ref[0]),
                             jnp.where(fwd, sa_ref[1], sd_ref[1])], axis=1)
        y = (lax.dot_general(u, m_ref[gi], _QK_DIMS, preferred_element_type=F32)
             + lax.dot_general(s.astype(BF16), q_ref[gi], _QK_DIMS, preferred_element_type=F32))
        y_ref[gi] = jax.nn.gelu(d_ref[gi] * u.astype(F32) + y)
        hf_ref[gi, :, 0:half] = jnp.where(fwd, fa[0], fd[0])
        hf_ref[gi, :, half:] = jnp.where(fwd, fa[1], fd[1])
        return carry

    lax.fori_loop(0, u_ref.shape[0], group, 0)


def _s5_chunks(u_g, m_w, p_w, q_w, a_l, d_l, h0, *, p_seqs, p_steps, s_seqs, s_steps):
    g, rows, tile = u_g.shape
    gb = S5_GROUPS_PER_STEP
    blk = lambda r, c: pl.BlockSpec((gb, r, c), lambda i: (i, 0, 0))
    return pl.pallas_call(
        functools.partial(_s5_chunk_kernel, p_seqs=p_seqs, p_steps=p_steps, s_seqs=s_seqs,
                          s_steps=s_steps),
        grid=(g // gb,),
        in_specs=[blk(rows, tile), blk(tile, tile), blk(tile, tile), blk(tile, tile),
                  blk(1, tile), blk(1, tile), blk(s_seqs, tile)],
        out_specs=[blk(rows, tile), blk(p_seqs, tile)],
        out_shape=[jax.ShapeDtypeStruct((g, rows, tile), F32),
                   jax.ShapeDtypeStruct((g, p_seqs, tile), F32)],
        scratch_shapes=[pltpu.VMEM((2, rows, tile // 2), F32)] * 3,
        compiler_params=_params("parallel"),
        name="s5_chunks",
    )(u_g, m_w, p_w, q_w, a_l, d_l, h0)


def _s5_glu_kernel(yg_ref, w_ref, b_ref, o_ref, y_ref):
    n_rows = yg_ref.shape[1]
    n_cols = y_ref.shape[0]
    groups_per_vreg = LANES // C_GROUP
    for t in range(S5_CHUNK):
        cols = slice(t * C_GROUP, (t + 1) * C_GROUP)
        for col in range(n_cols):
            piece = jnp.concatenate(
                [yg_ref[col * groups_per_vreg + j, :, cols] for j in range(groups_per_vreg)], axis=1)
            y_ref[col, pl.ds(t, n_rows, stride=S5_CHUNK), :] = piece
    y = jnp.concatenate([y_ref[col] for col in range(n_cols)], axis=1)
    z = jnp.dot(y.astype(BF16), w_ref[0], preferred_element_type=F32) + b_ref[...]
    o_ref[...] = (y * jax.nn.sigmoid(z)).astype(o_ref.dtype)


def _s5_glu(y_g, glu_w_stack, o, glu_b, *, tm):
    g, n_chunks, tile = y_g.shape
    width = g * C_GROUP
    m = n_chunks * S5_CHUNK
    return pl.pallas_call(
        _s5_glu_kernel,
        grid=(m // tm,),
        in_specs=[pl.BlockSpec((g, tm // S5_CHUNK, tile), lambda i: (0, i, 0)),
                  pl.BlockSpec((1, width, width), lambda i: (o, 0, 0)),
                  pl.BlockSpec((1, width), lambda i: (0, 0))],
        out_specs=pl.BlockSpec((tm, width), lambda i: (i, 0)),
        out_shape=jax.ShapeDtypeStruct((m, width), BF16),
        scratch_shapes=[pltpu.VMEM((width // LANES, tm, LANES), F32)],
        compiler_params=_params("parallel"),
        name="s5_ungroup_glu",
    )(y_g, glu_w_stack, glu_b.reshape(1, width))


def _s5_mixer(proj, state0, a_re, a_im, log_dt, b, c, d_skip, *, batch, seq, dec_batch, dec_seq):
    nd, g, p = a_re.shape
    h = b.shape[-1]
    lc = S5_CHUNK
    m_w, p_w, q_w, a_l = _s5_chunk_weights(a_re, a_im, log_dt, b, c)
    u_g = _s5_regroup(proj, width=g * h, rows_per_step=lc * lc)
    d_l = jnp.tile(d_skip.reshape(g, 1, h), (1, 1, lc))
    h0 = state0.transpose(3, 0, 2, 1, 4).reshape(g, dec_batch, 4 * p)
    y_g, hf = _s5_chunks(u_g, m_w, p_w, q_w, a_l, d_l, h0, p_seqs=batch, p_steps=seq // lc,
                         s_seqs=dec_batch, s_steps=dec_seq // lc)
    state = hf.reshape(g, batch, 2, 2, p).transpose(1, 3, 2, 0, 4)
    return y_g, state


def _final_norm_kernel(x_ref, g_ref, o_ref):
    o_ref[...] = _rms(x_ref[...]) * g_ref[...]


def _final_norm(x, g, *, row0, n_rows, tm):
    d = x.shape[1]
    b0 = row0 // tm
    return pl.pallas_call(
        _final_norm_kernel,
        grid=(n_rows // tm,),
        in_specs=[pl.BlockSpec((tm, d), lambda i: (b0 + i, 0)),
                  pl.BlockSpec((1, d), lambda i: (0, 0))],
        out_specs=pl.BlockSpec((tm, d), lambda i: (i, 0)),
        out_shape=jax.ShapeDtypeStruct((n_rows, d), F32),
        compiler_params=_params("parallel"),
        name="final_rms_norm",
    )(x, g.reshape(1, d))


def kernel(x_prompt, x_sample, c, cache_a_k, cache_a_v, cache_d_k, cache_d_v, state_c_ssm, c_ctx,
           ada_w, ada_b, norm_g, mlp_w1, mlp_w2, even_w_in, even_w_out, diff_lambda, diff_subln,
           conv_w, conv_b, conv_ln, odd_w_in, odd_w_out, ssm_a_re, ssm_a_im, ssm_log_dt, ssm_b,
           ssm_c, ssm_d, ssm_glu_w, ssm_glu_b, qk_norm, final_norm):
    batch, seq, d_model = x_prompt.shape
    dec_batch, dec_seq, _ = x_sample.shape
    depth = ada_w.shape[0]
    n_even = even_w_in.shape[0]
    n_odd = odd_w_in.shape[0]
    n_p = batch * seq
    n_s = dec_batch * dec_seq
    n_rows = n_p + n_s
    rows_kw = dict(n_prompt_rows=n_p, dec_seq=dec_seq)
    a_qk = A_HEADS * 2 * A_DK
    a_width = A_HEADS * A_DV
    c_width = ssm_d.shape[1]
    assert dec_seq % seq == 0 and 1 + dec_batch <= SUBLANES

    x = jnp.concatenate([x_prompt.reshape(n_p, d_model), x_sample.reshape(n_s, d_model)], axis=0)
    cond8 = jnp.concatenate([c_ctx[None, :], c,
                             jnp.zeros((SUBLANES - 1 - dec_batch, d_model), F32)], axis=0)
    mods = _ada(cond8, ada_w, ada_b).reshape(depth, SUBLANES, 1, N_MOD * d_model)

    rope_a = _rope_tables(dec_seq, A_DK, 2)
    rope_d = _rope_tables(dec_seq, D_HEAD_DIM, 1)
    w1_b, w2_b = mlp_w1.astype(BF16), mlp_w2.astype(BF16)
    even_in_b, even_out_b = even_w_in.astype(BF16), even_w_out.astype(BF16)
    odd_in_b, odd_out_b = odd_w_in.astype(BF16), odd_w_out.astype(BF16)
    glu_w_b = ssm_glu_w.astype(BF16)

    a_kv, d_kv, c_st = None, None, []
    for l in range(depth):
        if l % 2 == 0:
            e = l // 2
            lam_init = 0.8 - 0.6 * math.exp(-0.3 * l)
            proj = _norm_mm(x, norm_g[l, 0], mods, l, 0, even_in_b, e, relu2=False,
                            out_dtype=F32, tm=1024, tn=1024, **rows_kw)
            att, a_k, a_v = _dattn_prompt(proj, diff_lambda[e], diff_subln[e], lam_init, e, n_even,
                                          a_kv, batch=batch, seq=seq, n_rows=n_rows)
            a_kv = (a_k, a_v)
            att = _dattn_sample(proj, cache_a_k, cache_a_v, e, rope_a, diff_lambda[e],
                                diff_subln[e], lam_init, att, dec_batch=dec_batch, dec_seq=dec_seq,
                                n_prompt_rows=n_p)
            conv = _conv(proj, conv_w[e], conv_b[e], conv_ln[e], col0=2 * a_qk + a_width,
                         n_prompt_rows=n_p, prompt_seq=seq, dec_seq=dec_seq)
            x = _mm_res([att, conv], even_out_b, e, x, mods, l, 2,
                        tm=1024, tn=1024, tk=a_width, **rows_kw)
        else:
            o = l // 2
            proj = _norm_mm(x, norm_g[l, 0], mods, l, 0, odd_in_b, o, relu2=False,
                            out_dtype=F32, tm=1024, tn=1280, **rows_kw)
            y_g, state = _s5_mixer(proj, state_c_ssm[:, o], ssm_a_re[o], ssm_a_im[o],
                                   ssm_log_dt[o], ssm_b[o], ssm_c[o], ssm_d[o], batch=batch,
                                   seq=seq, dec_batch=dec_batch, dec_seq=dec_seq)
            ssm_out = _s5_glu(y_g, glu_w_b, o, ssm_glu_b[o], tm=1024)
            att, d_k, d_v = _gqa_prompt(proj, qk_norm[o], o, n_odd, d_kv, batch=batch, seq=seq,
                                        n_rows=n_rows, c_width=c_width)
            d_kv = (d_k, d_v)
            att = _gqa_sample(proj, cache_d_k, cache_d_v, o, rope_d, qk_norm[o], att,
                              dec_batch=dec_batch, dec_seq=dec_seq, n_prompt_rows=n_p,
                              c_width=c_width)
            x = _mm_res([ssm_out, att], odd_out_b, o, x, mods, l, 2,
                        tm=1024, tn=1024, tk=c_width, **rows_kw)
            c_st.append(state)
        hidden = _norm_mm(x, norm_g[l, 1], mods, l, 3, w1_b, l, relu2=True,
                          out_dtype=BF16, tm=1024, tn=1024, **rows_kw)
        x = _mm_res([hidden], w2_b, l, x, mods, l, 5, tm=1024, tn=512, tk=4096, **rows_kw)

    y_prompt = _final_norm(x, final_norm, row0=0, n_rows=n_p, tm=512).reshape(batch, seq, d_model)
    y_sample = _final_norm(x, final_norm, row0=n_p, n_rows=n_s, tm=512)
    y_sample = y_sample.reshape(dec_batch, dec_seq, d_model)
    return (y_prompt, y_sample, a_kv[0], a_kv[1], d_kv[0], d_kv[1], jnp.stack(c_st, axis=1))
```

```python
import functools
import math

import jax
import jax.numpy as jnp
from jax import lax
from jax.experimental import pallas as pl
from jax.experimental.pallas import tpu as pltpu

F32 = jnp.float32
BF16 = jnp.bfloat16

EPS = 1e-6
ROPE_BASE = 10000.0
GRID_W = 64
N_MOD = 6
A_HEADS = 8
A_DK = 64
A_DV = 2 * A_DK
B_KERNEL = 31
B_PAD = (B_KERNEL - 1) // 2
C_GROUP = 16
C_STATE = 64
D_HEADS = 8
D_KV_HEADS = 2
D_REP = D_HEADS // D_KV_HEADS
D_HEAD_DIM = 128

LANES = 128
SUBLANES = 8
VMEM_LIMIT = 56 * 1024 * 1024
NORM_ROWS = 16
CAST_BLOCKS = 64

S5_CHUNK = 16
S5_TILE = S5_CHUNK * C_GROUP
S5_GROUPS_PER_STEP = 4
S5_LAG_GROUPS_PER_STEP = 8


def _params(*sem):
    return pltpu.CompilerParams(dimension_semantics=sem, vmem_limit_bytes=VMEM_LIMIT)


def _mod_row(row0, n_prompt_rows, dec_seq):
    return jnp.maximum(row0 + (dec_seq - n_prompt_rows), 0) // dec_seq


def _ada_kernel(c_ref, w_ref, b_ref, o_ref):
    c = c_ref[...]
    s = (c * jax.nn.sigmoid(c)).astype(BF16)
    o_ref[0] = jnp.dot(s, w_ref[0].astype(BF16), preferred_element_type=F32) + b_ref[0]


def _ada(cond8, ada_w, ada_b):
    depth, d, n = ada_w.shape
    tn = 1024
    return pl.pallas_call(
        _ada_kernel,
        grid=(depth, n // tn),
        in_specs=[pl.BlockSpec((SUBLANES, d), lambda l, j: (0, 0)),
                  pl.BlockSpec((1, d, tn), lambda l, j: (l, 0, j)),
                  pl.BlockSpec((1, 1, tn), lambda l, j: (l, 0, j))],
        out_specs=pl.BlockSpec((1, SUBLANES, tn), lambda l, j: (l, 0, j)),
        out_shape=jax.ShapeDtypeStruct((depth, SUBLANES, n), F32),
        compiler_params=_params("parallel", "parallel"),
        name="ada_modulation",
    )(cond8, ada_w, ada_b.reshape(depth, 1, n))


def _rms(x):
    return x * lax.rsqrt(jnp.mean(x * x, axis=-1, keepdims=True) + EPS)


def _norm_mm_kernel(x_ref, g_ref, sh_ref, sc_ref, w_ref, *refs, relu2, n_cast):
    cast_in, o_ref, cast_out, h_ref = refs[:n_cast], refs[n_cast], refs[n_cast + 1:-1], refs[-1]

    @pl.when(pl.program_id(1) == 0)
    def _():
        gain = g_ref[...] * (1.0 + sc_ref[0, 0])
        shift = sh_ref[0, 0]

        def rows_body(r, carry):
            rows = pl.ds(pl.multiple_of(r * NORM_ROWS, NORM_ROWS), NORM_ROWS)
            h_ref[rows, :] = (_rms(x_ref[rows, :]) * gain + shift).astype(BF16)
            return carry

        lax.fori_loop(0, x_ref.shape[0] // NORM_ROWS, rows_body, 0, unroll=4)

    acc = jnp.dot(h_ref[...], w_ref[0], preferred_element_type=F32)
    if relu2:
        acc = jnp.maximum(acc, 0.0)
        acc = acc * acc
    o_ref[...] = acc.astype(o_ref.dtype)
    for src_ref, dst_ref in zip(cast_in, cast_out):
        dst_ref[...] = src_ref[0].astype(dst_ref.dtype)


def _norm_mm(x, g, mods, layer, chunk, w_stack, w_layer, *, relu2, out_dtype, tm, tn,
             n_prompt_rows, dec_seq, cast=()):
    m, d = x.shape
    n = w_stack.shape[2]
    ni, nj = m // tm, n // tn
    assert ni * nj >= CAST_BLOCKS or not cast
    row = lambda i: _mod_row(i * tm, n_prompt_rows, dec_seq)
    blk = lambda i, j: jnp.minimum(i * nj + j, CAST_BLOCKS - 1)
    cast_in_specs, cast_out_specs, cast_shapes = [], [], []
    for stack, cl in cast:
        _, r, c = stack.shape
        cast_in_specs.append(pl.BlockSpec((1, r // CAST_BLOCKS, c), lambda i, j, cl=cl: (cl, blk(i, j), 0)))
        cast_out_specs.append(pl.BlockSpec((r // CAST_BLOCKS, c), lambda i, j: (blk(i, j), 0)))
        cast_shapes.append(jax.ShapeDtypeStruct((r, c), BF16))
    outs = pl.pallas_call(
        functools.partial(_norm_mm_kernel, relu2=relu2, n_cast=len(cast)),
        grid=(ni, nj),
        in_specs=[pl.BlockSpec((tm, d), lambda i, j: (i, 0)),
                  pl.BlockSpec((1, d), lambda i, j: (0, 0)),
                  pl.BlockSpec((1, 1, 1, d), lambda i, j: (layer, row(i), 0, chunk)),
                  pl.BlockSpec((1, 1, 1, d), lambda i, j: (layer, row(i), 0, chunk + 1)),
                  pl.BlockSpec((1, d, tn), lambda i, j: (w_layer, 0, j))] + cast_in_specs,
        out_specs=[pl.BlockSpec((tm, tn), lambda i, j: (i, j))] + cast_out_specs,
        out_shape=[jax.ShapeDtypeStruct((m, n), out_dtype)] + cast_shapes,
        scratch_shapes=[pltpu.VMEM((tm, d), BF16)],
        compiler_params=_params("arbitrary", "arbitrary"),
        name="norm_mod_matmul",
    )(x, g.reshape(1, d), mods, mods, w_stack, *[stack for stack, _ in cast])
    return outs[0], [w[None] for w in outs[1:]]


def _mm_res_kernel(*refs, n_parts, nk):
    a_refs = refs[:n_parts]
    w_refs = refs[n_parts:2 * n_parts]
    res_ref, gate_ref, o_ref = refs[2 * n_parts:2 * n_parts + 3]
    part = jnp.dot(a_refs[0][...], w_refs[0][0], preferred_element_type=F32)
    for a_ref, w_ref in zip(a_refs[1:], w_refs[1:]):
        part = part + jnp.dot(a_ref[...], w_ref[0], preferred_element_type=F32)
    if nk == 1:
        o_ref[...] = res_ref[...] + gate_ref[0, 0] * part
        return
    acc_ref = refs[-1]
    k = pl.program_id(2)

    @pl.when(k == 0)
    def _():
        acc_ref[...] = part

    @pl.when(k > 0)
    def _():
        acc_ref[...] += part

    @pl.when(k == nk - 1)
    def _():
        o_ref[...] = res_ref[...] + gate_ref[0, 0] * acc_ref[...]


def _mm_res(a_parts, w_stack, w_layer, res, mods, layer, chunk, *, tm, tn, tk, n_prompt_rows,
            dec_seq):
    m, n = res.shape
    n_parts = len(a_parts)
    kdim = a_parts[0].shape[1]
    nk = kdim // tk
    gate_blk = chunk * (n // tn)
    row = lambda i: _mod_row(i * tm, n_prompt_rows, dec_seq)
    w_spec = lambda p: pl.BlockSpec((1, tk, tn), lambda i, j, k: (w_layer, p * nk + k, j))
    in_specs = ([pl.BlockSpec((tm, tk), lambda i, j, k: (i, k)) for _ in a_parts]
                + [w_spec(p) for p in range(n_parts)]
                + [pl.BlockSpec((tm, tn), lambda i, j, k: (i, j)),
                   pl.BlockSpec((1, 1, 1, tn), lambda i, j, k: (layer, row(i), 0, gate_blk + j))])
    scratch = [pltpu.VMEM((tm, tn), F32)] if nk > 1 else []
    return pl.pallas_call(
        functools.partial(_mm_res_kernel, n_parts=n_parts, nk=nk),
        grid=(m // tm, n // tn, nk),
        in_specs=in_specs,
        out_specs=pl.BlockSpec((tm, tn), lambda i, j, k: (i, j)),
        out_shape=jax.ShapeDtypeStruct((m, n), F32),
        scratch_shapes=scratch,
        compiler_params=_params("parallel", "parallel", "arbitrary"),
        name="matmul_gated_residual",
    )(*a_parts, *([w_stack] * n_parts), res, mods)


def _softmax(s):
    e = jnp.exp(s - jnp.max(s, axis=-1, keepdims=True))
    return e / jnp.sum(e, axis=-1, keepdims=True)


_QK_DIMS = (((1,), (1,)), ((), ()))


def _rope(x, cos, sin_up, sin_dn, quarter):
    width = x.shape[-1]
    return (x * cos + pltpu.roll(x, width - quarter, axis=1) * sin_up
            + pltpu.roll(x, quarter, axis=1) * sin_dn)


def _rope_tables(n_tokens, dim, reps):
    rows = n_tokens // GRID_W
    row = jnp.repeat(jnp.arange(rows, dtype=F32), GRID_W)
    col = jnp.tile(jnp.arange(GRID_W, dtype=F32), rows)
    quarter = dim // 4
    inv_freq = ROPE_BASE ** (-jnp.arange(quarter, dtype=F32) / quarter)
    ang_r = row[:, None] * inv_freq[None, :]
    ang_c = col[:, None] * inv_freq[None, :]
    ang = jnp.concatenate([ang_r, ang_r, ang_c, ang_c], axis=-1)
    cos, sin = jnp.cos(ang), jnp.sin(ang)
    even_chunk = ((jnp.arange(dim) // quarter) % 2 == 0)[None, :]
    sin_up = jnp.where(even_chunk, -sin, 0.0)
    sin_dn = jnp.where(even_chunk, 0.0, sin)
    tile = lambda t: jnp.tile(t, (1, reps))
    return tile(cos), tile(sin_up), tile(sin_dn)


def _diff_lambda(lam_ref, lam_init):
    lp = lam_ref[...]
    a = jnp.sum(lp[0:1] * lp[1:2], axis=-1, keepdims=True)
    b = jnp.sum(lp[2:3] * lp[3:4], axis=-1, keepdims=True)
    return jnp.exp(a) - jnp.exp(b) + lam_init


def _diff_attend(q, kb, vb, lam, subln, lam_init):
    lane = lax.broadcasted_iota(jnp.int32, q.shape, 1)
    q1 = jnp.where(lane < A_DK, q, 0.0).astype(BF16)
    q2 = jnp.where(lane >= A_DK, q, 0.0).astype(BF16)
    scale = A_DK ** -0.5
    s1 = lax.dot_general(q1, kb, _QK_DIMS, preferred_element_type=F32) * scale
    s2 = lax.dot_general(q2, kb, _QK_DIMS, preferred_element_type=F32) * scale
    p = _softmax(s1) - lam * _softmax(s2)
    o = jnp.dot(p.astype(BF16), vb, preferred_element_type=F32)
    return _rms(o) * subln * (1.0 - lam_init)


def _dattn_prompt_kernel(q_ref, k_ref, v_ref, lam_ref, sub_ref, *refs, lam_init):
    o_ref, ko_ref, vo_ref = refs[-3:]
    lam = _diff_lambda(lam_ref, lam_init)
    for h in range(A_HEADS):
        cols = slice(h * A_DV, (h + 1) * A_DV)
        k = k_ref[:, cols]
        v = v_ref[:, cols]
        ko_ref[0, 0, h] = k
        vo_ref[0, 0, h] = v
        o = _diff_attend(q_ref[:, cols], k.astype(BF16), v.astype(BF16), lam, sub_ref[...],
                         lam_init)
        o_ref[:, cols] = o.astype(o_ref.dtype)


def _dattn_prompt(proj, lam_p, subln, lam_init, e, n_even, kv_prev, *, batch, seq, n_rows):
    hd = 2 * A_DK
    width = A_HEADS * hd
    kv_shape = jax.ShapeDtypeStruct((batch, n_even, A_HEADS, seq, hd), F32)
    kv_spec = pl.BlockSpec((1, 1, A_HEADS, seq, hd), lambda b: (b, e, 0, 0, 0))
    extra, extra_specs, aliases = [], [], {}
    if kv_prev is not None:
        extra = list(kv_prev)
        extra_specs = [pl.BlockSpec(memory_space=pl.ANY)] * 2
        aliases = {5: 1, 6: 2}
    return pl.pallas_call(
        functools.partial(_dattn_prompt_kernel, lam_init=lam_init),
        grid=(batch,),
        in_specs=[pl.BlockSpec((seq, width), lambda b: (b, 0)),
                  pl.BlockSpec((seq, width), lambda b: (b, 1)),
                  pl.BlockSpec((seq, width), lambda b: (b, 2)),
                  pl.BlockSpec(lam_p.shape, lambda b: (0, 0)),
                  pl.BlockSpec((1, hd), lambda b: (0, 0))] + extra_specs,
        out_specs=[pl.BlockSpec((seq, width), lambda b: (b, 0)), kv_spec, kv_spec],
        out_shape=[jax.ShapeDtypeStruct((n_rows, width), BF16), kv_shape, kv_shape],
        input_output_aliases=aliases,
        compiler_params=_params("parallel"),
        name="diff_attention_prompt",
    )(proj, proj, proj, lam_p, subln.reshape(1, hd), *extra)


def _dattn_sample_kernel(q_ref, k_ref, v_ref, ck_ref, cv_ref, cos_ref, su_ref, sd_ref, lam_ref,
                         sub_ref, att_in_ref, o_ref, kf_ref, vf_ref, *, lam_init, tq):
    del att_in_ref
    t = k_ref.shape[0]
    quarter = A_DK // 4
    kf_ref[0:t] = _rope(k_ref[...], cos_ref[...], su_ref[...], sd_ref[...], quarter).astype(BF16)
    kf_ref[t:] = ck_ref[0, 0, 0].astype(BF16)
    vf_ref[0:t] = v_ref[...].astype(BF16)
    vf_ref[t:] = cv_ref[0, 0, 0].astype(BF16)
    lam = _diff_lambda(lam_ref, lam_init)
    for i in range(t // tq):
        rows = slice(i * tq, (i + 1) * tq)
        q = _rope(q_ref[rows, :], cos_ref[rows, :], su_ref[rows, :], sd_ref[rows, :], quarter)
        o = _diff_attend(q, kf_ref[...], vf_ref[...], lam, sub_ref[...], lam_init)
        o_ref[rows, :] = o.astype(o_ref.dtype)


def _dattn_sample(proj, cache_k, cache_v, e, rope, lam_p, subln, lam_init, att, *,
                  dec_batch, dec_seq, n_prompt_rows):
    hd = 2 * A_DK
    past = cache_k.shape[3]
    rb = n_prompt_rows // dec_seq
    cache_spec = pl.BlockSpec((1, 1, 1, past, hd), lambda b, h: (b, e, h, 0, 0))
    tab_spec = pl.BlockSpec((dec_seq, hd), lambda b, h: (0, 0))
    return pl.pallas_call(
        functools.partial(_dattn_sample_kernel, lam_init=lam_init, tq=256),
        grid=(dec_batch, A_HEADS),
        in_specs=[pl.BlockSpec((dec_seq, hd), lambda b, h: (rb + b, h)),
                  pl.BlockSpec((dec_seq, hd), lambda b, h: (rb + b, A_HEADS + h)),
                  pl.BlockSpec((dec_seq, hd), lambda b, h: (rb + b, 2 * A_HEADS + h)),
                  cache_spec, cache_spec, tab_spec, tab_spec, tab_spec,
                  pl.BlockSpec(lam_p.shape, lambda b, h: (0, 0)),
                  pl.BlockSpec((1, hd), lambda b, h: (0, 0)),
                  pl.BlockSpec(memory_space=pl.ANY)],
        out_specs=pl.BlockSpec((dec_seq, hd), lambda b, h: (rb + b, h)),
        out_shape=jax.ShapeDtypeStruct(att.shape, att.dtype),
        scratch_shapes=[pltpu.VMEM((dec_seq + past, hd), BF16),
                        pltpu.VMEM((dec_seq + past, hd), BF16)],
        input_output_aliases={10: 0},
        compiler_params=_params("parallel", "parallel"),
        name="diff_attention_sample",
    )(proj, proj, proj, cache_k, cache_v, *rope, lam_p, subln.reshape(1, hd), att)


def _gqa_attend(q, kb, vb):
    s = lax.dot_general(q.astype(BF16), kb, _QK_DIMS, preferred_element_type=F32) * (D_HEAD_DIM ** -0.5)
    return jnp.dot(_softmax(s).astype(BF16), vb, preferred_element_type=F32)


def _gqa_prompt_kernel(q_ref, k_ref, v_ref, g_ref, *refs):
    o_ref, ko_ref, vo_ref = refs[-3:]
    gq = g_ref[0:1]
    gk = g_ref[1:2]
    for g in range(D_KV_HEADS):
        kv_cols = slice(g * D_HEAD_DIM, (g + 1) * D_HEAD_DIM)
        k = _rms(k_ref[:, kv_cols]) * gk
        v = v_ref[:, kv_cols]
        ko_ref[0, 0, g] = k
        vo_ref[0, 0, g] = v
        kb = k.astype(BF16)
        vb = v.astype(BF16)
        for r in range(D_REP):
            head = g * D_REP + r
            cols = slice(head * D_HEAD_DIM, (head + 1) * D_HEAD_DIM)
            q = _rms(q_ref[:, cols]) * gq
            o_ref[:, cols] = _gqa_attend(q, kb, vb).astype(o_ref.dtype)


def _gqa_prompt(proj, qk_g, o, n_odd, kv_prev, *, batch, seq, n_rows, c_width):
    hd = D_HEAD_DIM
    qw = D_HEADS * hd
    kvw = D_KV_HEADS * hd
    q0 = c_width // qw
    k0 = (c_width + qw) // kvw
    kv_shape = jax.ShapeDtypeStruct((batch, n_odd, D_KV_HEADS, seq, hd), F32)
    kv_spec = pl.BlockSpec((1, 1, D_KV_HEADS, seq, hd), lambda b: (b, o, 0, 0, 0))
    extra, extra_specs, aliases = [], [], {}
    if kv_prev is not None:
        extra = list(kv_prev)
        extra_specs = [pl.BlockSpec(memory_space=pl.ANY)] * 2
        aliases = {4: 1, 5: 2}
    return pl.pallas_call(
        _gqa_prompt_kernel,
        grid=(batch,),
        in_specs=[pl.BlockSpec((seq, qw), lambda b: (b, q0)),
                  pl.BlockSpec((seq, kvw), lambda b: (b, k0)),
                  pl.BlockSpec((seq, kvw), lambda b: (b, k0 + 1)),
                  pl.BlockSpec(qk_g.shape, lambda b: (0, 0))] + extra_specs,
        out_specs=[pl.BlockSpec((seq, qw), lambda b: (b, 0)), kv_spec, kv_spec],
        out_shape=[jax.ShapeDtypeStruct((n_rows, qw), BF16), kv_shape, kv_shape],
        input_output_aliases=aliases,
        compiler_params=_params("parallel"),
        name="gqa_prompt",
    )(proj, proj, proj, qk_g, *extra)


def _gqa_sample_kernel(q_ref, k_ref, v_ref, ck_ref, cv_ref, cos_ref, su_ref, sd_ref, g_ref,
                       att_in_ref, o_ref, kf_ref, vf_ref, *, tq):
    del att_in_ref
    t = k_ref.shape[0]
    quarter = D_HEAD_DIM // 4
    gq = g_ref[0:1]
    k = _rms(k_ref[...]) * g_ref[1:2]
    kf_ref[0:t] = _rope(k, cos_ref[...], su_ref[...], sd_ref[...], quarter).astype(BF16)
    kf_ref[t:] = ck_ref[0, 0, 0].astype(BF16)
    vf_ref[0:t] = v_ref[...].astype(BF16)
    vf_ref[t:] = cv_ref[0, 0, 0].astype(BF16)
    for r in range(D_REP):
        cols = slice(r * D_HEAD_DIM, (r + 1) * D_HEAD_DIM)
        for i in range(t // tq):
            rows = slice(i * tq, (i + 1) * tq)
            q = _rms(q_ref[rows, cols]) * gq
            q = _rope(q, cos_ref[rows, :], su_ref[rows, :], sd_ref[rows, :], quarter)
            o_ref[rows, cols] = _gqa_attend(q, kf_ref[...], vf_ref[...]).astype(o_ref.dtype)


def _gqa_sample(proj, cache_k, cache_v, o, rope, qk_g, att, *, dec_batch, dec_seq, n_prompt_rows,
                c_width):
    hd = D_HEAD_DIM
    qw = D_REP * hd
    q0 = c_width // qw
    k0 = (c_width + D_HEADS * hd) // hd
    v0 = k0 + D_KV_HEADS
    past = cache_k.shape[3]
    rb = n_prompt_rows // dec_seq
    cache_spec = pl.BlockSpec((1, 1, 1, past, hd), lambda b, g: (b, o, g, 0, 0))
    tab_spec = pl.BlockSpec((dec_seq, hd), lambda b, g: (0, 0))
    return pl.pallas_call(
        functools.partial(_gqa_sample_kernel, tq=256),
        grid=(dec_batch, D_KV_HEADS),
        in_specs=[pl.BlockSpec((dec_seq, qw), lambda b, g: (rb + b, q0 + g)),
                  pl.BlockSpec((dec_seq, hd), lambda b, g: (rb + b, k0 + g)),
                  pl.BlockSpec((dec_seq, hd), lambda b, g: (rb + b, v0 + g)),
                  cache_spec, cache_spec, tab_spec, tab_spec, tab_spec,
                  pl.BlockSpec(qk_g.shape, lambda b, g: (0, 0)),
                  pl.BlockSpec(memory_space=pl.ANY)],
        out_specs=pl.BlockSpec((dec_seq, qw), lambda b, g: (rb + b, g)),
        out_shape=jax.ShapeDtypeStruct(att.shape, att.dtype),
        scratch_shapes=[pltpu.VMEM((dec_seq + past, hd), BF16),
                        pltpu.VMEM((dec_seq + past, hd), BF16)],
        input_output_aliases={9: 0},
        compiler_params=_params("parallel", "parallel"),
        name="gqa_sample",
    )(proj, proj, proj, cache_k, cache_v, *rope, qk_g, att)


CONV_ROWS = 256
CONV_HALO = 16


def _conv_kernel(a_ref, g_ref, ap_ref, gp_ref, an_ref, gn_ref, w_ref, b_ref, lng_ref, lnb_ref,
                 o_ref, pad_ref, cv_ref, *, n_prompt_blocks, blocks_per_seq):
    rows, width = a_ref.shape
    n_chunks = width // LANES
    i = pl.program_id(0)
    j = jnp.maximum(i - n_prompt_blocks, 0) % blocks_per_seq
    latent = i >= n_prompt_blocks
    has_prev = jnp.where(jnp.logical_and(latent, j > 0), 1.0, 0.0)
    has_next = jnp.where(jnp.logical_and(latent, j < blocks_per_seq - 1), 1.0, 0.0)

    glu = lambda a, g: a * jax.nn.sigmoid(g)
    h_prev = glu(ap_ref[...], gp_ref[...]) * has_prev
    h_cur = glu(a_ref[...], g_ref[...])
    h_next = glu(an_ref[...], gn_ref[...]) * has_next
    for c in range(n_chunks):
        lanes = slice(c * LANES, (c + 1) * LANES)
        pad_ref[c, 0:CONV_HALO, :] = h_prev[:, lanes]
        pad_ref[c, CONV_HALO:CONV_HALO + rows, :] = h_cur[:, lanes]
        pad_ref[c, CONV_HALO + rows:, :] = h_next[:, lanes]

    def chunk_body(c, carry):
        for r in range(rows // SUBLANES):
            base = CONV_HALO - B_PAD + r * SUBLANES
            acc = w_ref[c, 0] * pad_ref[c, pl.ds(base, SUBLANES), :]
            for tap in range(1, B_KERNEL):
                acc = acc + w_ref[c, tap] * pad_ref[c, pl.ds(base + tap, SUBLANES), :]
            cv_ref[c, pl.ds(r * SUBLANES, SUBLANES), :] = acc
        return carry

    lax.fori_loop(0, n_chunks, chunk_body, 0)

    x = jnp.concatenate([cv_ref[c] for c in range(n_chunks)], axis=1) + b_ref[...]
    mu = jnp.mean(x, axis=-1, keepdims=True)
    xc = x - mu
    var = jnp.mean(xc * xc, axis=-1, keepdims=True)
    y = xc * lax.rsqrt(var + EPS) * lng_ref[...] + lnb_ref[...]
    o_ref[...] = (y * jax.nn.sigmoid(y)).astype(o_ref.dtype)


def _conv(proj, conv_w, conv_b, conv_ln, *, col0, n_prompt_rows, prompt_seq, dec_seq):
    m = proj.shape[0]
    taps, width = conv_w.shape
    assert prompt_seq == CONV_ROWS and dec_seq % CONV_ROWS == 0
    n_chunks = width // LANES
    a_blk = col0 // width
    halo_per_blk = CONV_ROWS // CONV_HALO
    last_halo = m // CONV_HALO - 1
    prev_map = lambda cb: (lambda i: (jnp.maximum(i * halo_per_blk - 1, 0), cb))
    next_map = lambda cb: (lambda i: (jnp.minimum((i + 1) * halo_per_blk, last_halo), cb))
    w8 = jnp.broadcast_to(conv_w.reshape(taps, 1, n_chunks, LANES), (taps, SUBLANES, n_chunks, LANES))
    w8 = w8.transpose(2, 0, 1, 3)
    row_spec = lambda: pl.BlockSpec((1, width), lambda i: (0, 0))
    return pl.pallas_call(
        functools.partial(_conv_kernel, n_prompt_blocks=n_prompt_rows // CONV_ROWS,
                          blocks_per_seq=dec_seq // CONV_ROWS),
        grid=(m // CONV_ROWS,),
        in_specs=[pl.BlockSpec((CONV_ROWS, width), lambda i: (i, a_blk)),
                  pl.BlockSpec((CONV_ROWS, width), lambda i: (i, a_blk + 1)),
                  pl.BlockSpec((CONV_HALO, width), prev_map(a_blk)),
                  pl.BlockSpec((CONV_HALO, width), prev_map(a_blk + 1)),
                  pl.BlockSpec((CONV_HALO, width), next_map(a_blk)),
                  pl.BlockSpec((CONV_HALO, width), next_map(a_blk + 1)),
                  pl.BlockSpec(w8.shape, lambda i: (0, 0, 0, 0)),
                  row_spec(), row_spec(), row_spec()],
        out_specs=pl.BlockSpec((CONV_ROWS, width), lambda i: (i, 0)),
        out_shape=jax.ShapeDtypeStruct((m, width), BF16),
        scratch_shapes=[pltpu.VMEM((n_chunks, CONV_ROWS + 2 * CONV_HALO, LANES), F32),
                        pltpu.VMEM((n_chunks, CONV_ROWS, LANES), F32)],
        compiler_params=_params("parallel"),
        name="conformer_conv",
    )(proj, proj, proj, proj, proj, proj, w8, conv_b.reshape(1, width),
      conv_ln[0].reshape(1, width), conv_ln[1].reshape(1, width))


def _s5_weights_kernel(ar_ref, ai_ref, ldt_ref, btr_ref, bti_ref, cr_ref, ci_ref,
                       mt_ref, p_ref, qt_ref, al_ref):
    n_lag = S5_CHUNK
    hi = lax.Precision.HIGHEST
    disc = []
    for d in range(2):
        ar = ar_ref[d]
        ai = ai_ref[d]
        dt = jnp.exp(ldt_ref[d])
        mag = jnp.exp(ar * dt)
        abr = mag * jnp.cos(ai * dt)
        abi = mag * jnp.sin(ai * dt)
        den = ar * ar + ai * ai
        nr = abr - 1.0
        ni = abi
        disc.append((abr, abi, (nr * ar + ni * ai) / den, (ni * ar - nr * ai) / den))
    for gi in range(ar_ref.shape[1]):
        row = slice(gi, gi + 1)
        tables = []
        for d in range(2):
            abr, abi, kr, ki = disc[d]
            a_r, a_i = abr[row], abi[row]
            bbr = kr[row] * btr_ref[d, gi] - ki[row] * bti_ref[d, gi]
            bbi = kr[row] * bti_ref[d, gi] + ki[row] * btr_ref[d, gi]
            cr = cr_ref[d, gi]
            ci = ci_ref[d, gi]
            pr, pi = jnp.ones_like(a_r), jnp.zeros_like(a_r)
            wt_r, wt_i, v_r, v_i = [], [], [], []
            for lag in range(n_lag + 1):
                if lag < n_lag:
                    wt_r.append(pr * bbr - pi * bbi)
                    wt_i.append(pr * bbi + pi * bbr)
                if lag >= 1:
                    v_r.append(cr * pr - ci * pi)
                    v_i.append(cr * pi + ci * pr)
                if lag < n_lag:
                    pr, pi = pr * a_r - pi * a_i, pr * a_i + pi * a_r
            if d == 0:
                wt_r.reverse()
                wt_i.reverse()
            else:
                v_r.reverse()
                v_i.reverse()
            wtr = jnp.concatenate(wt_r, axis=0)
            wti = jnp.concatenate(wt_i, axis=0)
            klag = (lax.dot_general(cr, wtr, _QK_DIMS, precision=hi, preferred_element_type=F32)
                    - lax.dot_general(ci, wti, _QK_DIMS, precision=hi, preferred_element_type=F32))
            tables.append((wtr, wti, jnp.concatenate(v_r, axis=0), jnp.concatenate(v_i, axis=0),
                           klag, pr, pi))
        (wtr_f, wti_f, vr_f, vi_f, k_f, alr_f, ali_f), (wtr_b, wti_b, vr_b, vi_b, k_b, alr_b, ali_b) = tables
        p_ref[gi] = jnp.concatenate([wtr_f, wtr_b, wti_f, wti_b], axis=1).astype(p_ref.dtype)
        qt_ref[gi] = jnp.concatenate([vr_f, vr_b, -vi_f, -vi_b], axis=1).astype(qt_ref.dtype)
        al_ref[gi] = jnp.concatenate([alr_f, alr_b, ali_f, ali_b], axis=1)
        h = k_f.shape[0]
        blocks = []
        for t in range(n_lag):
            lo = (n_lag - 1 - t) * h
            fwd = k_f if lo == 0 else jnp.concatenate([k_f[:, lo:], jnp.zeros((h, lo), F32)], axis=1)
            bwd = k_b if t == 0 else jnp.concatenate(
                [jnp.zeros((h, t * h), F32), k_b[:, :(n_lag - t) * h]], axis=1)
            blocks.append(fwd + bwd)
        mt_ref[gi] = jnp.concatenate(blocks, axis=0).astype(mt_ref.dtype)


def _s5_chunk_weights(a_re, a_im, log_dt, b, c):
    nd, g, p = a_re.shape
    h = b.shape[-1]
    gb = S5_LAG_GROUPS_PER_STEP
    tile = S5_CHUNK * h
    bt = b.transpose(0, 1, 2, 4, 3)
    ldt = jnp.broadcast_to(log_dt[:, :, None], (nd, g, p))
    vec = pl.BlockSpec((nd, gb, p), lambda i: (0, i, 0))
    mat = pl.BlockSpec((nd, gb, h, p), lambda i: (0, i, 0, 0))
    sq = pl.BlockSpec((gb, tile, tile), lambda i: (i, 0, 0))
    sq_shape = jax.ShapeDtypeStruct((g, tile, tile), BF16)
    return pl.pallas_call(
        _s5_weights_kernel,
        grid=(g // gb,),
        in_specs=[vec, vec, vec, mat, mat, mat, mat],
        out_specs=[sq, sq, sq, pl.BlockSpec((gb, 1, 4 * p), lambda i: (i, 0, 0))],
        out_shape=[sq_shape, sq_shape, sq_shape, jax.ShapeDtypeStruct((g, 1, 4 * p), F32)],
        compiler_params=_params("parallel"),
        name="s5_chunk_weights",
    )(a_re, a_im, ldt, bt[:, 0], bt[:, 1], c[:, 0], c[:, 1])


def _s5_regroup_kernel(*refs):
    x_refs, o_ref = refs[:-1], refs[-1]
    n_chunks = x_refs[0].shape[0] // S5_CHUNK
    groups_per_vreg = LANES // C_GROUP
    for col, x_ref in enumerate(x_refs):
        xs = [x_ref[pl.ds(s, n_chunks, stride=S5_CHUNK), :] for s in range(S5_CHUNK)]
        for j in range(groups_per_vreg):
            cols = slice(j * C_GROUP, (j + 1) * C_GROUP)
            row = jnp.concatenate([x[:, cols] for x in xs], axis=1)
            o_ref[col * groups_per_vreg + j] = row.astype(o_ref.dtype)


def _s5_regroup(proj, *, width, rows_per_step):
    m = proj.shape[0]
    g = width // C_GROUP
    n_cols = width // LANES
    chunks_per_step = rows_per_step // S5_CHUNK
    col_spec = lambda c: pl.BlockSpec((rows_per_step, LANES), lambda i: (i, c))
    return pl.pallas_call(
        _s5_regroup_kernel,
        grid=(m // rows_per_step,),
        in_specs=[col_spec(c) for c in range(n_cols)],
        out_specs=pl.BlockSpec((g, chunks_per_step, S5_TILE), lambda i: (0, i, 0)),
        out_shape=jax.ShapeDtypeStruct((g, m // S5_CHUNK, S5_TILE), BF16),
        compiler_params=_params("parallel"),
        name="s5_regroup",
    )(*([proj] * n_cols))


def _s5_chunk_kernel(u_ref, m_ref, p_ref, q_ref, al_ref, d_ref, h0_ref, y_ref, hf_ref, x_ref,
                     sa_ref, sd_ref, *, p_seqs, p_steps, s_seqs, s_steps):
    half = x_ref.shape[2]
    lane = lax.broadcasted_iota(jnp.int32, (1, half), 1)
    fwd = lane < (half // 2)

    def scan(out_ref, gi, row0, seqs, steps, sr, si, reverse):
        alr = al_ref[gi, :, 0:half]
        ali = al_ref[gi, :, half:]
        for c in (range(steps - 1, -1, -1) if reverse else range(steps)):
            rs = pl.ds(row0 + c, seqs, stride=steps)
            out_ref[0, rs, :] = sr
            out_ref[1, rs, :] = si
            xr = x_ref[0, rs, :]
            xi = x_ref[1, rs, :]
            sr, si = alr * sr - ali * si + xr, alr * si + ali * sr + xi
        return sr, si

    def group(gi, carry):
        u = u_ref[gi]
        x = jnp.dot(u, p_ref[gi], preferred_element_type=F32)
        x_ref[0] = x[:, 0:half]
        x_ref[1] = x[:, half:]
        zero = jnp.zeros((p_seqs, half), F32)
        fa = scan(sa_ref, gi, 0, p_seqs, p_steps, zero, zero, False)
        fd = scan(sd_ref, gi, 0, p_seqs, p_steps, zero, zero, True)
        h0r = h0_ref[gi, :, 0:half]
        h0i = h0_ref[gi, :, half:]
        s_row0 = p_seqs * p_steps
        scan(sa_ref, gi, s_row0, s_seqs, s_steps, h0r, h0i, False)
        scan(sd_ref, gi, s_row0, s_seqs, s_steps, h0r, h0i, True)
        s = jnp.concatenate([jnp.where(fwd, sa_ref[0], sd_ref[0]),
                             jnp.where(fwd, sa_ref[1], sd_ref[1])], axis=1)
        y = (lax.dot_general(u, m_ref[gi], _QK_DIMS, preferred_element_type=F32)
             + lax.dot_general(s.astype(BF16), q_ref[gi], _QK_DIMS, preferred_element_type=F32))
        y_ref[gi] = jax.nn.gelu(d_ref[gi] * u.astype(F32) + y)
        hf_ref[gi, :, 0:half] = jnp.where(fwd, fa[0], fd[0])
        hf_ref[gi, :, half:] = jnp.where(fwd, fa[1], fd[1])
        return carry

    lax.fori_loop(0, u_ref.shape[0], group, 0)


def _s5_chunks(u_g, m_w, p_w, q_w, a_l, d_l, h0, *, p_seqs, p_steps, s_seqs, s_steps):
    g, rows, tile = u_g.shape
    gb = S5_GROUPS_PER_STEP
    blk = lambda r, c: pl.BlockSpec((gb, r, c), lambda i: (i, 0, 0))
    return pl.pallas_call(
        functools.partial(_s5_chunk_kernel, p_seqs=p_seqs, p_steps=p_steps, s_seqs=s_seqs,
                          s_steps=s_steps),
        grid=(g // gb,),
        in_specs=[blk(rows, tile), blk(tile, tile), blk(tile, tile), blk(tile, tile),
                  blk(1, tile), blk(1, tile), blk(s_seqs, tile)],
        out_specs=[blk(rows, tile), blk(p_seqs, tile)],
        out_shape=[jax.ShapeDtypeStruct((g, rows, tile), F32),
                   jax.ShapeDtypeStruct((g, p_seqs, tile), F32)],
        scratch_shapes=[pltpu.VMEM((2, rows, tile // 2), F32)] * 3,
        compiler_params=_params("parallel"),
        name="s5_chunks",
    )(u_g, m_w, p_w, q_w, a_l, d_l, h0)


def _s5_glu_kernel(yg_ref, w_ref, b_ref, o_ref, y_ref):
    n_rows = yg_ref.shape[1]
    n_cols = y_ref.shape[0]
    groups_per_vreg = LANES // C_GROUP
    for t in range(S5_CHUNK):
        cols = slice(t * C_GROUP, (t + 1) * C_GROUP)
        for col in range(n_cols):
            piece = jnp.concatenate(
                [yg_ref[col * groups_per_vreg + j, :, cols] for j in range(groups_per_vreg)], axis=1)
            y_ref[col, pl.ds(t, n_rows, stride=S5_CHUNK), :] = piece
    y = jnp.concatenate([y_ref[col] for col in range(n_cols)], axis=1)
    z = jnp.dot(y.astype(BF16), w_ref[0], preferred_element_type=F32) + b_ref[...]
    o_ref[...] = (y * jax.nn.sigmoid(z)).astype(o_ref.dtype)


def _s5_glu(y_g, glu_w_stack, o, glu_b, *, tm):
    g, n_chunks, tile = y_g.shape
    width = g * C_GROUP
    m = n_chunks * S5_CHUNK
    return pl.pallas_call(
        _s5_glu_kernel,
        grid=(m // tm,),
        in_specs=[pl.BlockSpec((g, tm // S5_CHUNK, tile), lambda i: (0, i, 0)),
                  pl.BlockSpec((1, width, width), lambda i: (o, 0, 0)),
                  pl.BlockSpec((1, width), lambda i: (0, 0))],
        out_specs=pl.BlockSpec((tm, width), lambda i: (i, 0)),
        out_shape=jax.ShapeDtypeStruct((m, width), BF16),
        scratch_shapes=[pltpu.VMEM((width // LANES, tm, LANES), F32)],
        compiler_params=_params("parallel"),
        name="s5_ungroup_glu",
    )(y_g, glu_w_stack, glu_b.reshape(1, width))


def _s5_mixer(proj, state0, a_re, a_im, log_dt, b, c, d_skip, *, batch, seq, dec_batch, dec_seq):
    nd, g, p = a_re.shape
    h = b.shape[-1]
    lc = S5_CHUNK
    m_w, p_w, q_w, a_l = _s5_chunk_weights(a_re, a_im, log_dt, b, c)
    u_g = _s5_regroup(proj, width=g * h, rows_per_step=lc * lc)
    d_l = jnp.tile(d_skip.reshape(g, 1, h), (1, 1, lc))
    h0 = state0.transpose(3, 0, 2, 1, 4).reshape(g, dec_batch, 4 * p)
    y_g, hf = _s5_chunks(u_g, m_w, p_w, q_w, a_l, d_l, h0, p_seqs=batch, p_steps=seq // lc,
                         s_seqs=dec_batch, s_steps=dec_seq // lc)
    state = hf.reshape(g, batch, 2, 2, p).transpose(1, 3, 2, 0, 4)
    return y_g, state


def _final_norm_kernel(x_ref, g_ref, o_ref):
    o_ref[...] = _rms(x_ref[...]) * g_ref[...]


def _final_norm(x, g, *, row0, n_rows, tm):
    d = x.shape[1]
    b0 = row0 // tm
    return pl.pallas_call(
        _final_norm_kernel,
        grid=(n_rows // tm,),
        in_specs=[pl.BlockSpec((tm, d), lambda i: (b0 + i, 0)),
                  pl.BlockSpec((1, d), lambda i: (0, 0))],
        out_specs=pl.BlockSpec((tm, d), lambda i: (i, 0)),
        out_shape=jax.ShapeDtypeStruct((n_rows, d), F32),
        compiler_params=_params("parallel"),
        name="final_rms_norm",
    )(x, g.reshape(1, d))


def kernel(x_prompt, x_sample, c, cache_a_k, cache_a_v, cache_d_k, cache_d_v, state_c_ssm, c_ctx,
           ada_w, ada_b, norm_g, mlp_w1, mlp_w2, even_w_in, even_w_out, diff_lambda, diff_subln,
           conv_w, conv_b, conv_ln, odd_w_in, odd_w_out, ssm_a_re, ssm_a_im, ssm_log_dt, ssm_b,
           ssm_c, ssm_d, ssm_glu_w, ssm_glu_b, qk_norm, final_norm):
    batch, seq, d_model = x_prompt.shape
    dec_batch, dec_seq, _ = x_sample.shape
    depth = ada_w.shape[0]
    n_even = even_w_in.shape[0]
    n_odd = odd_w_in.shape[0]
    n_p = batch * seq
    n_s = dec_batch * dec_seq
    n_rows = n_p + n_s
    rows_kw = dict(n_prompt_rows=n_p, dec_seq=dec_seq)
    a_qk = A_HEADS * 2 * A_DK
    a_width = A_HEADS * A_DV
    c_width = ssm_d.shape[1]
    assert dec_seq % seq == 0 and 1 + dec_batch <= SUBLANES

    x = jnp.concatenate([x_prompt.reshape(n_p, d_model), x_sample.reshape(n_s, d_model)], axis=0)
    cond8 = jnp.concatenate([c_ctx[None, :], c,
                             jnp.zeros((SUBLANES - 1 - dec_batch, d_model), F32)], axis=0)
    mods = _ada(cond8, ada_w, ada_b).reshape(depth, SUBLANES, 1, N_MOD * d_model)

    rope_a = _rope_tables(dec_seq, A_DK, 2)
    rope_d = _rope_tables(dec_seq, D_HEAD_DIM, 1)
    mlp_b = [mlp_w1[:1].astype(BF16), mlp_w2[:1].astype(BF16)]
    even_in_b, even_out_b = even_w_in.astype(BF16), even_w_out.astype(BF16)
    odd_in_b, odd_out_b = odd_w_in.astype(BF16), odd_w_out.astype(BF16)
    glu_w_b = ssm_glu_w.astype(BF16)

    a_kv, d_kv, c_st = None, None, []
    for l in range(depth):
        if l % 2 == 0:
            e = l // 2
            lam_init = 0.8 - 0.6 * math.exp(-0.3 * l)
            proj, _ = _norm_mm(x, norm_g[l, 0], mods, l, 0, even_in_b, e, relu2=False,
                               out_dtype=F32, tm=1024, tn=1024, **rows_kw)
            att, a_k, a_v = _dattn_prompt(proj, diff_lambda[e], diff_subln[e], lam_init, e, n_even,
                                          a_kv, batch=batch, seq=seq, n_rows=n_rows)
            a_kv = (a_k, a_v)
            att = _dattn_sample(proj, cache_a_k, cache_a_v, e, rope_a, diff_lambda[e],
                                diff_subln[e], lam_init, att, dec_batch=dec_batch, dec_seq=dec_seq,
                                n_prompt_rows=n_p)
            conv = _conv(proj, conv_w[e], conv_b[e], conv_ln[e], col0=2 * a_qk + a_width,
                         n_prompt_rows=n_p, prompt_seq=seq, dec_seq=dec_seq)
            x = _mm_res([att, conv], even_out_b, e, x, mods, l, 2,
                        tm=1024, tn=1024, tk=a_width, **rows_kw)
        else:
            o = l // 2
            proj, _ = _norm_mm(x, norm_g[l, 0], mods, l, 0, odd_in_b, o, relu2=False,
                               out_dtype=F32, tm=1024, tn=1280, **rows_kw)
            y_g, state = _s5_mixer(proj, state_c_ssm[:, o], ssm_a_re[o], ssm_a_im[o],
                                   ssm_log_dt[o], ssm_b[o], ssm_c[o], ssm_d[o], batch=batch,
                                   seq=seq, dec_batch=dec_batch, dec_seq=dec_seq)
            ssm_out = _s5_glu(y_g, glu_w_b, o, ssm_glu_b[o], tm=1024)
            att, d_k, d_v = _gqa_prompt(proj, qk_norm[o], o, n_odd, d_kv, batch=batch, seq=seq,
                                        n_rows=n_rows, c_width=c_width)
            d_kv = (d_k, d_v)
            att = _gqa_sample(proj, cache_d_k, cache_d_v, o, rope_d, qk_norm[o], att,
                              dec_batch=dec_batch, dec_seq=dec_seq, n_prompt_rows=n_p,
                              c_width=c_width)
            x = _mm_res([ssm_out, att], odd_out_b, o, x, mods, l, 2,
                        tm=1024, tn=1024, tk=c_width, **rows_kw)
            c_st.append(state)
        w1_b, w2_b = mlp_b
        ride_along = [(mlp_w1, l + 1), (mlp_w2, l + 1)] if l + 1 < depth else []
        hidden, mlp_b = _norm_mm(x, norm_g[l, 1], mods, l, 3, w1_b, 0, relu2=True,
                                 out_dtype=BF16, tm=1024, tn=1024, cast=ride_along, **rows_kw)
        x = _mm_res([hidden], w2_b, 0, x, mods, l, 5, tm=1024, tn=1024, tk=2048, **rows_kw)

    y_prompt = _final_norm(x, final_norm, row0=0, n_rows=n_p, tm=512).reshape(batch, seq, d_model)
    y_sample = _final_norm(x, final_norm, row0=n_p, n_rows=n_s, tm=512)
    y_sample = y_sample.reshape(dec_batch, dec_seq, d_model)
    return (y_prompt, y_sample, a_kv[0], a_kv[1], d_kv[0], d_kv[1], jnp.stack(c_st, axis=1))
```

```python
import functools
import math

import jax
import jax.numpy as jnp
from jax import lax
from jax.experimental import pallas as pl
from jax.experimental.pallas import tpu as pltpu

F32 = jnp.float32
BF16 = jnp.bfloat16

EPS = 1e-6
ROPE_BASE = 10000.0
GRID_W = 64
N_MOD = 6
A_HEADS = 8
A_DK = 64
A_DV = 2 * A_DK
B_KERNEL = 31
B_PAD = (B_KERNEL - 1) // 2
C_GROUP = 16
C_STATE = 64
D_HEADS = 8
D_KV_HEADS = 2
D_REP = D_HEADS // D_KV_HEADS
D_HEAD_DIM = 128

LANES = 128
SUBLANES = 8
VMEM_LIMIT = 56 * 1024 * 1024
NORM_ROWS = 16
CAST_BLOCKS = 64

S5_CHUNK = 16
S5_TILE = S5_CHUNK * C_GROUP
S5_GROUPS_PER_STEP = 4
S5_LAG_GROUPS_PER_STEP = 8


def _params(*sem):
    return pltpu.CompilerParams(dimension_semantics=sem, vmem_limit_bytes=VMEM_LIMIT)


def _mod_row(row0, n_prompt_rows, dec_seq):
    return jnp.maximum(row0 + (dec_seq - n_prompt_rows), 0) // dec_seq


def _ada_kernel(c_ref, w_ref, b_ref, o_ref):
    c = c_ref[...]
    s = (c * jax.nn.sigmoid(c)).astype(BF16)
    o_ref[0] = jnp.dot(s, w_ref[0].astype(BF16), preferred_element_type=F32) + b_ref[0]


def _ada(cond8, ada_w, ada_b):
    depth, d, n = ada_w.shape
    tn = 1024
    return pl.pallas_call(
        _ada_kernel,
        grid=(depth, n // tn),
        in_specs=[pl.BlockSpec((SUBLANES, d), lambda l, j: (0, 0)),
                  pl.BlockSpec((1, d, tn), lambda l, j: (l, 0, j)),
                  pl.BlockSpec((1, 1, tn), lambda l, j: (l, 0, j))],
        out_specs=pl.BlockSpec((1, SUBLANES, tn), lambda l, j: (l, 0, j)),
        out_shape=jax.ShapeDtypeStruct((depth, SUBLANES, n), F32),
        compiler_params=_params("parallel", "parallel"),
        name="ada_modulation",
    )(cond8, ada_w, ada_b.reshape(depth, 1, n))


def _rms(x):
    return x * lax.rsqrt(jnp.mean(x * x, axis=-1, keepdims=True) + EPS)


def _norm_mm_kernel(x_ref, g_ref, sh_ref, sc_ref, w_ref, *refs, relu2, n_cast):
    cast_in, o_ref, cast_out, h_ref = refs[:n_cast], refs[n_cast], refs[n_cast + 1:-1], refs[-1]

    @pl.when(pl.program_id(1) == 0)
    def _():
        gain = g_ref[...] * (1.0 + sc_ref[0, 0])
        shift = sh_ref[0, 0]

        def rows_body(r, carry):
            rows = pl.ds(pl.multiple_of(r * NORM_ROWS, NORM_ROWS), NORM_ROWS)
            h_ref[rows, :] = (_rms(x_ref[rows, :]) * gain + shift).astype(BF16)
            return carry

        lax.fori_loop(0, x_ref.shape[0] // NORM_ROWS, rows_body, 0, unroll=4)

    acc = jnp.dot(h_ref[...], w_ref[0], preferred_element_type=F32)
    if relu2:
        acc = jnp.maximum(acc, 0.0)
        acc = acc * acc
    o_ref[...] = acc.astype(o_ref.dtype)
    for src_ref, dst_ref in zip(cast_in, cast_out):
        dst_ref[...] = src_ref[0].astype(dst_ref.dtype)


def _norm_mm(x, g, mods, layer, chunk, w_stack, w_layer, *, relu2, out_dtype, tm, tn,
             n_prompt_rows, dec_seq, cast=()):
    m, d = x.shape
    n = w_stack.shape[2]
    ni, nj = m // tm, n // tn
    assert ni * nj >= CAST_BLOCKS or not cast
    row = lambda i: _mod_row(i * tm, n_prompt_rows, dec_seq)
    blk = lambda i, j: jnp.minimum(i * nj + j, CAST_BLOCKS - 1)
    cast_in_specs, cast_out_specs, cast_shapes = [], [], []
    for stack, cl in cast:
        _, r, c = stack.shape
        cast_in_specs.append(pl.BlockSpec((1, r // CAST_BLOCKS, c), lambda i, j, cl=cl: (cl, blk(i, j), 0)))
        cast_out_specs.append(pl.BlockSpec((r // CAST_BLOCKS, c), lambda i, j: (blk(i, j), 0)))
        cast_shapes.append(jax.ShapeDtypeStruct((r, c), BF16))
    outs = pl.pallas_call(
        functools.partial(_norm_mm_kernel, relu2=relu2, n_cast=len(cast)),
        grid=(ni, nj),
        in_specs=[pl.BlockSpec((tm, d), lambda i, j: (i, 0)),
                  pl.BlockSpec((1, d), lambda i, j: (0, 0)),
                  pl.BlockSpec((1, 1, 1, d), lambda i, j: (layer, row(i), 0, chunk)),
                  pl.BlockSpec((1, 1, 1, d), lambda i, j: (layer, row(i), 0, chunk + 1)),
                  pl.BlockSpec((1, d, tn), lambda i, j: (w_layer, 0, j))] + cast_in_specs,
        out_specs=[pl.BlockSpec((tm, tn), lambda i, j: (i, j))] + cast_out_specs,
        out_shape=[jax.ShapeDtypeStruct((m, n), out_dtype)] + cast_shapes,
        scratch_shapes=[pltpu.VMEM((tm, d), BF16)],
        compiler_params=_params("arbitrary", "arbitrary"),
        name="norm_mod_matmul",
    )(x, g.reshape(1, d), mods, mods, w_stack, *[stack for stack, _ in cast])
    return outs[0], [w[None] for w in outs[1:]]


def _mm_res_kernel(*refs, n_parts, nk):
    a_refs = refs[:n_parts]
    w_refs = refs[n_parts:2 * n_parts]
    res_ref, gate_ref, o_ref = refs[2 * n_parts:2 * n_parts + 3]
    part = jnp.dot(a_refs[0][...], w_refs[0][0], preferred_element_type=F32)
    for a_ref, w_ref in zip(a_refs[1:], w_refs[1:]):
        part = part + jnp.dot(a_ref[...], w_ref[0], preferred_element_type=F32)
    if nk == 1:
        o_ref[...] = res_ref[...] + gate_ref[0, 0] * part
        return
    acc_ref = refs[-1]
    k = pl.program_id(2)

    @pl.when(k == 0)
    def _():
        acc_ref[...] = part

    @pl.when(k > 0)
    def _():
        acc_ref[...] += part

    @pl.when(k == nk - 1)
    def _():
        o_ref[...] = res_ref[...] + gate_ref[0, 0] * acc_ref[...]


def _mm_res(a_parts, w_stack, w_layer, res, mods, layer, chunk, *, tm, tn, tk, n_prompt_rows,
            dec_seq):
    m, n = res.shape
    n_parts = len(a_parts)
    kdim = a_parts[0].shape[1]
    nk = kdim // tk
    gate_blk = chunk * (n // tn)
    row = lambda i: _mod_row(i * tm, n_prompt_rows, dec_seq)
    w_spec = lambda p: pl.BlockSpec((1, tk, tn), lambda i, j, k: (w_layer, p * nk + k, j))
    in_specs = ([pl.BlockSpec((tm, tk), lambda i, j, k: (i, k)) for _ in a_parts]
                + [w_spec(p) for p in range(n_parts)]
                + [pl.BlockSpec((tm, tn), lambda i, j, k: (i, j)),
                   pl.BlockSpec((1, 1, 1, tn), lambda i, j, k: (layer, row(i), 0, gate_blk + j))])
    scratch = [pltpu.VMEM((tm, tn), F32)] if nk > 1 else []
    return pl.pallas_call(
        functools.partial(_mm_res_kernel, n_parts=n_parts, nk=nk),
        grid=(m // tm, n // tn, nk),
        in_specs=in_specs,
        out_specs=pl.BlockSpec((tm, tn), lambda i, j, k: (i, j)),
        out_shape=jax.ShapeDtypeStruct((m, n), F32),
        scratch_shapes=scratch,
        compiler_params=_params("parallel", "parallel", "arbitrary"),
        name="matmul_gated_residual",
    )(*a_parts, *([w_stack] * n_parts), res, mods)


def _softmax(s):
    e = jnp.exp(s - jnp.max(s, axis=-1, keepdims=True))
    return e / jnp.sum(e, axis=-1, keepdims=True)


_QK_DIMS = (((1,), (1,)), ((), ()))


def _rope(x, cos, sin_up, sin_dn, quarter):
    width = x.shape[-1]
    return (x * cos + pltpu.roll(x, width - quarter, axis=1) * sin_up
            + pltpu.roll(x, quarter, axis=1) * sin_dn)


def _rope_tables(n_tokens, dim, reps):
    rows = n_tokens // GRID_W
    row = jnp.repeat(jnp.arange(rows, dtype=F32), GRID_W)
    col = jnp.tile(jnp.arange(GRID_W, dtype=F32), rows)
    quarter = dim // 4
    inv_freq = ROPE_BASE ** (-jnp.arange(quarter, dtype=F32) / quarter)
    ang_r = row[:, None] * inv_freq[None, :]
    ang_c = col[:, None] * inv_freq[None, :]
    ang = jnp.concatenate([ang_r, ang_r, ang_c, ang_c], axis=-1)
    cos, sin = jnp.cos(ang), jnp.sin(ang)
    even_chunk = ((jnp.arange(dim) // quarter) % 2 == 0)[None, :]
    sin_up = jnp.where(even_chunk, -sin, 0.0)
    sin_dn = jnp.where(even_chunk, 0.0, sin)
    tile = lambda t: jnp.tile(t, (1, reps))
    return tile(cos), tile(sin_up), tile(sin_dn)


def _diff_lambda(lam_ref, lam_init):
    lp = lam_ref[...]
    a = jnp.sum(lp[0:1] * lp[1:2], axis=-1, keepdims=True)
    b = jnp.sum(lp[2:3] * lp[3:4], axis=-1, keepdims=True)
    return jnp.exp(a) - jnp.exp(b) + lam_init


def _diff_attend(q, kb, vb, lam, subln, lam_init):
    lane = lax.broadcasted_iota(jnp.int32, q.shape, 1)
    q1 = jnp.where(lane < A_DK, q, 0.0).astype(BF16)
    q2 = jnp.where(lane >= A_DK, q, 0.0).astype(BF16)
    scale = A_DK ** -0.5
    s1 = lax.dot_general(q1, kb, _QK_DIMS, preferred_element_type=F32) * scale
    s2 = lax.dot_general(q2, kb, _QK_DIMS, preferred_element_type=F32) * scale
    p = _softmax(s1) - lam * _softmax(s2)
    o = jnp.dot(p.astype(BF16), vb, preferred_element_type=F32)
    return _rms(o) * subln * (1.0 - lam_init)


def _dattn_prompt_kernel(q_ref, k_ref, v_ref, lam_ref, sub_ref, *refs, lam_init):
    o_ref, ko_ref, vo_ref = refs[-3:]
    lam = _diff_lambda(lam_ref, lam_init)
    for h in range(A_HEADS):
        cols = slice(h * A_DV, (h + 1) * A_DV)
        k = k_ref[:, cols]
        v = v_ref[:, cols]
        ko_ref[0, 0, h] = k
        vo_ref[0, 0, h] = v
        o = _diff_attend(q_ref[:, cols], k.astype(BF16), v.astype(BF16), lam, sub_ref[...],
                         lam_init)
        o_ref[:, cols] = o.astype(o_ref.dtype)


def _dattn_prompt(proj, lam_p, subln, lam_init, e, n_even, kv_prev, *, batch, seq, n_rows):
    hd = 2 * A_DK
    width = A_HEADS * hd
    kv_shape = jax.ShapeDtypeStruct((batch, n_even, A_HEADS, seq, hd), F32)
    kv_spec = pl.BlockSpec((1, 1, A_HEADS, seq, hd), lambda b: (b, e, 0, 0, 0))
    extra, extra_specs, aliases = [], [], {}
    if kv_prev is not None:
        extra = list(kv_prev)
        extra_specs = [pl.BlockSpec(memory_space=pl.ANY)] * 2
        aliases = {5: 1, 6: 2}
    return pl.pallas_call(
        functools.partial(_dattn_prompt_kernel, lam_init=lam_init),
        grid=(batch,),
        in_specs=[pl.BlockSpec((seq, width), lambda b: (b, 0)),
                  pl.BlockSpec((seq, width), lambda b: (b, 1)),
                  pl.BlockSpec((seq, width), lambda b: (b, 2)),
                  pl.BlockSpec(lam_p.shape, lambda b: (0, 0)),
                  pl.BlockSpec((1, hd), lambda b: (0, 0))] + extra_specs,
        out_specs=[pl.BlockSpec((seq, width), lambda b: (b, 0)), kv_spec, kv_spec],
        out_shape=[jax.ShapeDtypeStruct((n_rows, width), BF16), kv_shape, kv_shape],
        input_output_aliases=aliases,
        compiler_params=_params("parallel"),
        name="diff_attention_prompt",
    )(proj, proj, proj, lam_p, subln.reshape(1, hd), *extra)


def _dattn_sample_kernel(q_ref, k_ref, v_ref, ck_ref, cv_ref, cos_ref, su_ref, sd_ref, lam_ref,
                         sub_ref, att_in_ref, o_ref, kf_ref, vf_ref, *, lam_init, tq):
    del att_in_ref
    t = k_ref.shape[0]
    quarter = A_DK // 4
    kf_ref[0:t] = _rope(k_ref[...], cos_ref[...], su_ref[...], sd_ref[...], quarter).astype(BF16)
    kf_ref[t:] = ck_ref[0, 0, 0].astype(BF16)
    vf_ref[0:t] = v_ref[...].astype(BF16)
    vf_ref[t:] = cv_ref[0, 0, 0].astype(BF16)
    lam = _diff_lambda(lam_ref, lam_init)
    for i in range(t // tq):
        rows = slice(i * tq, (i + 1) * tq)
        q = _rope(q_ref[rows, :], cos_ref[rows, :], su_ref[rows, :], sd_ref[rows, :], quarter)
        o = _diff_attend(q, kf_ref[...], vf_ref[...], lam, sub_ref[...], lam_init)
        o_ref[rows, :] = o.astype(o_ref.dtype)


def _dattn_sample(proj, cache_k, cache_v, e, rope, lam_p, subln, lam_init, att, *,
                  dec_batch, dec_seq, n_prompt_rows):
    hd = 2 * A_DK
    past = cache_k.shape[3]
    rb = n_prompt_rows // dec_seq
    cache_spec = pl.BlockSpec((1, 1, 1, past, hd), lambda b, h: (b, e, h, 0, 0))
    tab_spec = pl.BlockSpec((dec_seq, hd), lambda b, h: (0, 0))
    return pl.pallas_call(
        functools.partial(_dattn_sample_kernel, lam_init=lam_init, tq=256),
        grid=(dec_batch, A_HEADS),
        in_specs=[pl.BlockSpec((dec_seq, hd), lambda b, h: (rb + b, h)),
                  pl.BlockSpec((dec_seq, hd), lambda b, h: (rb + b, A_HEADS + h)),
                  pl.BlockSpec((dec_seq, hd), lambda b, h: (rb + b, 2 * A_HEADS + h)),
                  cache_spec, cache_spec, tab_spec, tab_spec, tab_spec,
                  pl.BlockSpec(lam_p.shape, lambda b, h: (0, 0)),
                  pl.BlockSpec((1, hd), lambda b, h: (0, 0)),
                  pl.BlockSpec(memory_space=pl.ANY)],
        out_specs=pl.BlockSpec((dec_seq, hd), lambda b, h: (rb + b, h)),
        out_shape=jax.ShapeDtypeStruct(att.shape, att.dtype),
        scratch_shapes=[pltpu.VMEM((dec_seq + past, hd), BF16),
                        pltpu.VMEM((dec_seq + past, hd), BF16)],
        input_output_aliases={10: 0},
        compiler_params=_params("parallel", "parallel"),
        name="diff_attention_sample",
    )(proj, proj, proj, cache_k, cache_v, *rope, lam_p, subln.reshape(1, hd), att)


def _gqa_attend(q, kb, vb):
    s = lax.dot_general(q.astype(BF16), kb, _QK_DIMS, preferred_element_type=F32) * (D_HEAD_DIM ** -0.5)
    return jnp.dot(_softmax(s).astype(BF16), vb, preferred_element_type=F32)


def _gqa_prompt_kernel(q_ref, k_ref, v_ref, g_ref, *refs):
    o_ref, ko_ref, vo_ref = refs[-3:]
    gq = g_ref[0:1]
    gk = g_ref[1:2]
    for g in range(D_KV_HEADS):
        kv_cols = slice(g * D_HEAD_DIM, (g + 1) * D_HEAD_DIM)
        k = _rms(k_ref[:, kv_cols]) * gk
        v = v_ref[:, kv_cols]
        ko_ref[0, 0, g] = k
        vo_ref[0, 0, g] = v
        kb = k.astype(BF16)
        vb = v.astype(BF16)
        for r in range(D_REP):
            head = g * D_REP + r
            cols = slice(head * D_HEAD_DIM, (head + 1) * D_HEAD_DIM)
            q = _rms(q_ref[:, cols]) * gq
            o_ref[:, cols] = _gqa_attend(q, kb, vb).astype(o_ref.dtype)


def _gqa_prompt(proj, qk_g, o, n_odd, kv_prev, *, batch, seq, n_rows, c_width):
    hd = D_HEAD_DIM
    qw = D_HEADS * hd
    kvw = D_KV_HEADS * hd
    q0 = c_width // qw
    k0 = (c_width + qw) // kvw
    kv_shape = jax.ShapeDtypeStruct((batch, n_odd, D_KV_HEADS, seq, hd), F32)
    kv_spec = pl.BlockSpec((1, 1, D_KV_HEADS, seq, hd), lambda b: (b, o, 0, 0, 0))
    extra, extra_specs, aliases = [], [], {}
    if kv_prev is not None:
        extra = list(kv_prev)
        extra_specs = [pl.BlockSpec(memory_space=pl.ANY)] * 2
        aliases = {4: 1, 5: 2}
    return pl.pallas_call(
        _gqa_prompt_kernel,
        grid=(batch,),
        in_specs=[pl.BlockSpec((seq, qw), lambda b: (b, q0)),
                  pl.BlockSpec((seq, kvw), lambda b: (b, k0)),
                  pl.BlockSpec((seq, kvw), lambda b: (b, k0 + 1)),
                  pl.BlockSpec(qk_g.shape, lambda b: (0, 0))] + extra_specs,
        out_specs=[pl.BlockSpec((seq, qw), lambda b: (b, 0)), kv_spec, kv_spec],
        out_shape=[jax.ShapeDtypeStruct((n_rows, qw), BF16), kv_shape, kv_shape],
        input_output_aliases=aliases,
        compiler_params=_params("parallel"),
        name="gqa_prompt",
    )(proj, proj, proj, qk_g, *extra)


def _gqa_sample_kernel(q_ref, k_ref, v_ref, ck_ref, cv_ref, cos_ref, su_ref, sd_ref, g_ref,
                       att_in_ref, o_ref, kf_ref, vf_ref, *, tq):
    del att_in_ref
    t = k_ref.shape[0]
    quarter = D_HEAD_DIM // 4
    gq = g_ref[0:1]
    k = _rms(k_ref[...]) * g_ref[1:2]
    kf_ref[0:t] = _rope(k, cos_ref[...], su_ref[...], sd_ref[...], quarter).astype(BF16)
    kf_ref[t:] = ck_ref[0, 0, 0].astype(BF16)
    vf_ref[0:t] = v_ref[...].astype(BF16)
    vf_ref[t:] = cv_ref[0, 0, 0].astype(BF16)
    for r in range(D_REP):
        cols = slice(r * D_HEAD_DIM, (r + 1) * D_HEAD_DIM)
        for i in range(t // tq):
            rows = slice(i * tq, (i + 1) * tq)
            q = _rms(q_ref[rows, cols]) * gq
            q = _rope(q, cos_ref[rows, :], su_ref[rows, :], sd_ref[rows, :], quarter)
            o_ref[rows, cols] = _gqa_attend(q, kf_ref[...], vf_ref[...]).astype(o_ref.dtype)


def _gqa_sample(proj, cache_k, cache_v, o, rope, qk_g, att, *, dec_batch, dec_seq, n_prompt_rows,
                c_width):
    hd = D_HEAD_DIM
    qw = D_REP * hd
    q0 = c_width // qw
    k0 = (c_width + D_HEADS * hd) // hd
    v0 = k0 + D_KV_HEADS
    past = cache_k.shape[3]
    rb = n_prompt_rows // dec_seq
    cache_spec = pl.BlockSpec((1, 1, 1, past, hd), lambda b, g: (b, o, g, 0, 0))
    tab_spec = pl.BlockSpec((dec_seq, hd), lambda b, g: (0, 0))
    return pl.pallas_call(
        functools.partial(_gqa_sample_kernel, tq=256),
        grid=(dec_batch, D_KV_HEADS),
        in_specs=[pl.BlockSpec((dec_seq, qw), lambda b, g: (rb + b, q0 + g)),
                  pl.BlockSpec((dec_seq, hd), lambda b, g: (rb + b, k0 + g)),
                  pl.BlockSpec((dec_seq, hd), lambda b, g: (rb + b, v0 + g)),
                  cache_spec, cache_spec, tab_spec, tab_spec, tab_spec,
                  pl.BlockSpec(qk_g.shape, lambda b, g: (0, 0)),
                  pl.BlockSpec(memory_space=pl.ANY)],
        out_specs=pl.BlockSpec((dec_seq, qw), lambda b, g: (rb + b, g)),
        out_shape=jax.ShapeDtypeStruct(att.shape, att.dtype),
        scratch_shapes=[pltpu.VMEM((dec_seq + past, hd), BF16),
                        pltpu.VMEM((dec_seq + past, hd), BF16)],
        input_output_aliases={9: 0},
        compiler_params=_params("parallel", "parallel"),
        name="gqa_sample",
    )(proj, proj, proj, cache_k, cache_v, *rope, qk_g, att)


CONV_ROWS = 256
CONV_HALO = 16


def _conv_kernel(a_ref, g_ref, ap_ref, gp_ref, an_ref, gn_ref, w_ref, b_ref, lng_ref, lnb_ref,
                 o_ref, pad_ref, cv_ref, *, n_prompt_blocks, blocks_per_seq):
    rows, width = a_ref.shape
    n_chunks = width // LANES
    i = pl.program_id(0)
    j = jnp.maximum(i - n_prompt_blocks, 0) % blocks_per_seq
    latent = i >= n_prompt_blocks
    has_prev = jnp.where(jnp.logical_and(latent, j > 0), 1.0, 0.0)
    has_next = jnp.where(jnp.logical_and(latent, j < blocks_per_seq - 1), 1.0, 0.0)

    glu = lambda a, g: a * jax.nn.sigmoid(g)
    h_prev = glu(ap_ref[...], gp_ref[...]) * has_prev
    h_cur = glu(a_ref[...], g_ref[...])
    h_next = glu(an_ref[...], gn_ref[...]) * has_next
    for c in range(n_chunks):
        lanes = slice(c * LANES, (c + 1) * LANES)
        pad_ref[c, 0:CONV_HALO, :] = h_prev[:, lanes]
        pad_ref[c, CONV_HALO:CONV_HALO + rows, :] = h_cur[:, lanes]
        pad_ref[c, CONV_HALO + rows:, :] = h_next[:, lanes]

    def chunk_body(c, carry):
        for r in range(rows // SUBLANES):
            base = CONV_HALO - B_PAD + r * SUBLANES
            acc = w_ref[c, 0] * pad_ref[c, pl.ds(base, SUBLANES), :]
            for tap in range(1, B_KERNEL):
                acc = acc + w_ref[c, tap] * pad_ref[c, pl.ds(base + tap, SUBLANES), :]
            cv_ref[c, pl.ds(r * SUBLANES, SUBLANES), :] = acc
        return carry

    lax.fori_loop(0, n_chunks, chunk_body, 0)

    x = jnp.concatenate([cv_ref[c] for c in range(n_chunks)], axis=1) + b_ref[...]
    mu = jnp.mean(x, axis=-1, keepdims=True)
    xc = x - mu
    var = jnp.mean(xc * xc, axis=-1, keepdims=True)
    y = xc * lax.rsqrt(var + EPS) * lng_ref[...] + lnb_ref[...]
    o_ref[...] = (y * jax.nn.sigmoid(y)).astype(o_ref.dtype)


def _conv(proj, conv_w, conv_b, conv_ln, *, col0, n_prompt_rows, prompt_seq, dec_seq):
    m = proj.shape[0]
    taps, width = conv_w.shape
    assert prompt_seq == CONV_ROWS and dec_seq % CONV_ROWS == 0
    n_chunks = width // LANES
    a_blk = col0 // width
    halo_per_blk = CONV_ROWS // CONV_HALO
    last_halo = m // CONV_HALO - 1
    prev_map = lambda cb: (lambda i: (jnp.maximum(i * halo_per_blk - 1, 0), cb))
    next_map = lambda cb: (lambda i: (jnp.minimum((i + 1) * halo_per_blk, last_halo), cb))
    w8 = jnp.broadcast_to(conv_w.reshape(taps, 1, n_chunks, LANES), (taps, SUBLANES, n_chunks, LANES))
    w8 = w8.transpose(2, 0, 1, 3)
    row_spec = lambda: pl.BlockSpec((1, width), lambda i: (0, 0))
    return pl.pallas_call(
        functools.partial(_conv_kernel, n_prompt_blocks=n_prompt_rows // CONV_ROWS,
                          blocks_per_seq=dec_seq // CONV_ROWS),
        grid=(m // CONV_ROWS,),
        in_specs=[pl.BlockSpec((CONV_ROWS, width), lambda i: (i, a_blk)),
                  pl.BlockSpec((CONV_ROWS, width), lambda i: (i, a_blk + 1)),
                  pl.BlockSpec((CONV_HALO, width), prev_map(a_blk)),
                  pl.BlockSpec((CONV_HALO, width), prev_map(a_blk + 1)),
                  pl.BlockSpec((CONV_HALO, width), next_map(a_blk)),
                  pl.BlockSpec((CONV_HALO, width), next_map(a_blk + 1)),
                  pl.BlockSpec(w8.shape, lambda i: (0, 0, 0, 0)),
                  row_spec(), row_spec(), row_spec()],
        out_specs=pl.BlockSpec((CONV_ROWS, width), lambda i: (i, 0)),
        out_shape=jax.ShapeDtypeStruct((m, width), BF16),
        scratch_shapes=[pltpu.VMEM((n_chunks, CONV_ROWS + 2 * CONV_HALO, LANES), F32),
                        pltpu.VMEM((n_chunks, CONV_ROWS, LANES), F32)],
        compiler_params=_params("parallel"),
        name="conformer_conv",
    )(proj, proj, proj, proj, proj, proj, w8, conv_b.reshape(1, width),
      conv_ln[0].reshape(1, width), conv_ln[1].reshape(1, width))


def _s5_weights_kernel(ar_ref, ai_ref, ldt_ref, btr_ref, bti_ref, cr_ref, ci_ref,
                       mt_ref, p_ref, qt_ref, al_ref):
    n_lag = S5_CHUNK
    hi = lax.Precision.HIGHEST
    disc = []
    for d in range(2):
        ar = ar_ref[d]
        ai = ai_ref[d]
        dt = jnp.exp(ldt_ref[d])
        mag = jnp.exp(ar * dt)
        abr = mag * jnp.cos(ai * dt)
        abi = mag * jnp.sin(ai * dt)
        den = ar * ar + ai * ai
        nr = abr - 1.0
        ni = abi
        disc.append((abr, abi, (nr * ar + ni * ai) / den, (ni * ar - nr * ai) / den))
    for gi in range(ar_ref.shape[1]):
        row = slice(gi, gi + 1)
        tables = []
        for d in range(2):
            abr, abi, kr, ki = disc[d]
            a_r, a_i = abr[row], abi[row]
            bbr = kr[row] * btr_ref[d, gi] - ki[row] * bti_ref[d, gi]
            bbi = kr[row] * bti_ref[d, gi] + ki[row] * btr_ref[d, gi]
            cr = cr_ref[d, gi]
            ci = ci_ref[d, gi]
            pr, pi = jnp.ones_like(a_r), jnp.zeros_like(a_r)
            wt_r, wt_i, v_r, v_i = [], [], [], []
            for lag in range(n_lag + 1):
                if lag < n_lag:
                    wt_r.append(pr * bbr - pi * bbi)
                    wt_i.append(pr * bbi + pi * bbr)
                if lag >= 1:
                    v_r.append(cr * pr - ci * pi)
                    v_i.append(cr * pi + ci * pr)
                if lag < n_lag:
                    pr, pi = pr * a_r - pi * a_i, pr * a_i + pi * a_r
            if d == 0:
                wt_r.reverse()
                wt_i.reverse()
            else:
                v_r.reverse()
                v_i.reverse()
            wtr = jnp.concatenate(wt_r, axis=0)
            wti = jnp.concatenate(wt_i, axis=0)
            klag = (lax.dot_general(cr, wtr, _QK_DIMS, precision=hi, preferred_element_type=F32)
                    - lax.dot_general(ci, wti, _QK_DIMS, precision=hi, preferred_element_type=F32))
            tables.append((wtr, wti, jnp.concatenate(v_r, axis=0), jnp.concatenate(v_i, axis=0),
                           klag, pr, pi))
        (wtr_f, wti_f, vr_f, vi_f, k_f, alr_f, ali_f), (wtr_b, wti_b, vr_b, vi_b, k_b, alr_b, ali_b) = tables
        p_ref[gi] = jnp.concatenate([wtr_f, wtr_b, wti_f, wti_b], axis=1).astype(p_ref.dtype)
        qt_ref[gi] = jnp.concatenate([vr_f, vr_b, -vi_f, -vi_b], axis=1).astype(qt_ref.dtype)
        al_ref[gi] = jnp.concatenate([alr_f, alr_b, ali_f, ali_b], axis=1)
        h = k_f.shape[0]
        blocks = []
        for t in range(n_lag):
            lo = (n_lag - 1 - t) * h
            fwd = k_f if lo == 0 else jnp.concatenate([k_f[:, lo:], jnp.zeros((h, lo), F32)], axis=1)
            bwd = k_b if t == 0 else jnp.concatenate(
                [jnp.zeros((h, t * h), F32), k_b[:, :(n_lag - t) * h]], axis=1)
            blocks.append(fwd + bwd)
        mt_ref[gi] = jnp.concatenate(blocks, axis=0).astype(mt_ref.dtype)


def _s5_chunk_weights(a_re, a_im, log_dt, b, c):
    nd, g, p = a_re.shape
    h = b.shape[-1]
    gb = S5_LAG_GROUPS_PER_STEP
    tile = S5_CHUNK * h
    bt = b.transpose(0, 1, 2, 4, 3)
    ldt = jnp.broadcast_to(log_dt[:, :, None], (nd, g, p))
    vec = pl.BlockSpec((nd, gb, p), lambda i: (0, i, 0))
    mat = pl.BlockSpec((nd, gb, h, p), lambda i: (0, i, 0, 0))
    sq = pl.BlockSpec((gb, tile, tile), lambda i: (i, 0, 0))
    sq_shape = jax.ShapeDtypeStruct((g, tile, tile), BF16)
    return pl.pallas_call(
        _s5_weights_kernel,
        grid=(g // gb,),
        in_specs=[vec, vec, vec, mat, mat, mat, mat],
        out_specs=[sq, sq, sq, pl.BlockSpec((gb, 1, 4 * p), lambda i: (i, 0, 0))],
        out_shape=[sq_shape, sq_shape, sq_shape, jax.ShapeDtypeStruct((g, 1, 4 * p), F32)],
        compiler_params=_params("parallel"),
        name="s5_chunk_weights",
    )(a_re, a_im, ldt, bt[:, 0], bt[:, 1], c[:, 0], c[:, 1])


def _s5_regroup_kernel(*refs):
    x_refs, o_ref = refs[:-1], refs[-1]
    n_chunks = x_refs[0].shape[0] // S5_CHUNK
    groups_per_vreg = LANES // C_GROUP
    for col, x_ref in enumerate(x_refs):
        xs = [x_ref[pl.ds(s, n_chunks, stride=S5_CHUNK), :] for s in range(S5_CHUNK)]
        for j in range(groups_per_vreg):
            cols = slice(j * C_GROUP, (j + 1) * C_GROUP)
            row = jnp.concatenate([x[:, cols] for x in xs], axis=1)
            o_ref[col * groups_per_vreg + j] = row.astype(o_ref.dtype)


def _s5_regroup(proj, *, width, rows_per_step):
    m = proj.shape[0]
    g = width // C_GROUP
    n_cols = width // LANES
    chunks_per_step = rows_per_step // S5_CHUNK
    col_spec = lambda c: pl.BlockSpec((rows_per_step, LANES), lambda i: (i, c))
    return pl.pallas_call(
        _s5_regroup_kernel,
        grid=(m // rows_per_step,),
        in_specs=[col_spec(c) for c in range(n_cols)],
        out_specs=pl.BlockSpec((g, chunks_per_step, S5_TILE), lambda i: (0, i, 0)),
        out_shape=jax.ShapeDtypeStruct((g, m // S5_CHUNK, S5_TILE), BF16),
        compiler_params=_params("parallel"),
        name="s5_regroup",
    )(*([proj] * n_cols))


def _s5_chunk_kernel(u_ref, m_ref, p_ref, q_ref, al_ref, d_ref, h0_ref, y_ref, hf_ref, x_ref,
                     sa_ref, sd_ref, *, p_seqs, p_steps, s_seqs, s_steps):
    half = x_ref.shape[2]
    lane = lax.broadcasted_iota(jnp.int32, (1, half), 1)
    fwd = lane < (half // 2)

    def scan(out_ref, gi, row0, seqs, steps, sr, si, reverse):
        alr = al_ref[gi, :, 0:half]
        ali = al_ref[gi, :, half:]
        for c in (range(steps - 1, -1, -1) if reverse else range(steps)):
            rs = pl.ds(row0 + c, seqs, stride=steps)
            out_ref[0, rs, :] = sr
            out_ref[1, rs, :] = si
            xr = x_ref[0, rs, :]
            xi = x_ref[1, rs, :]
            sr, si = alr * sr - ali * si + xr, alr * si + ali * sr + xi
        return sr, si

    def group(gi, carry):
        u = u_ref[gi]
        x = jnp.dot(u, p_ref[gi], preferred_element_type=F32)
        x_ref[0] = x[:, 0:half]
        x_ref[1] = x[:, half:]
        zero = jnp.zeros((p_seqs, half), F32)
        fa = scan(sa_ref, gi, 0, p_seqs, p_steps, zero, zero, False)
        fd = scan(sd_ref, gi, 0, p_seqs, p_steps, zero, zero, True)
        h0r = h0_ref[gi, :, 0:half]
        h0i = h0_ref[gi, :, half:]
        s_row0 = p_seqs * p_steps
        scan(sa_ref, gi, s_row0, s_seqs, s_steps, h0r, h0i, False)
        scan(sd_ref, gi, s_row0, s_seqs, s_steps, h0r, h0i, True)
        s = jnp.concatenate([jnp.where(fwd, sa_ref[0], sd_ref[0]),
                             jnp.where(fwd, sa_ref[1], sd_ref[1])], axis=1)
        y = (lax.dot_general(u, m_ref[gi], _QK_DIMS, preferred_element_type=F32)
             + lax.dot_general(s.astype(BF16), q_ref[gi], _QK_DIMS, preferred_element_type=F32))
        y_ref[gi] = jax.nn.gelu(d_ref[gi] * u.astype(F32) + y)
        hf_ref[gi, :, 0:half] = jnp.where(fwd, fa[0], fd[0])
        hf_ref[gi, :, half:] = jnp.where(fwd, fa[1], fd[1])
        return carry

    lax.fori_loop(0, u_ref.shape[0], group, 0)


def _s5_chunks(u_g, m_w, p_w, q_w, a_l, d_l, h0, *, p_seqs, p_steps, s_seqs, s_steps):
    g, rows, tile = u_g.shape
    gb = S5_GROUPS_PER_STEP
    blk = lambda r, c: pl.BlockSpec((gb, r, c), lambda i: (i, 0, 0))
    return pl.pallas_call(
        functools.partial(_s5_chunk_kernel, p_seqs=p_seqs, p_steps=p_steps, s_seqs=s_seqs,
                          s_steps=s_steps),
        grid=(g // gb,),
        in_specs=[blk(rows, tile), blk(tile, tile), blk(tile, tile), blk(tile, tile),
                  blk(1, tile), blk(1, tile), blk(s_seqs, tile)],
        out_specs=[blk(rows, tile), blk(p_seqs, tile)],
        out_shape=[jax.ShapeDtypeStruct((g, rows, tile), F32),
                   jax.ShapeDtypeStruct((g, p_seqs, tile), F32)],
        scratch_shapes=[pltpu.VMEM((2, rows, tile // 2), F32)] * 3,
        compiler_params=_params("parallel"),
        name="s5_chunks",
    )(u_g, m_w, p_w, q_w, a_l, d_l, h0)


def _s5_glu_kernel(yg_ref, w_ref, b_ref, o_ref, y_ref):
    n_rows = yg_ref.shape[1]
    n_cols = y_ref.shape[0]
    groups_per_vreg = LANES // C_GROUP
    for t in range(S5_CHUNK):
        cols = slice(t * C_GROUP, (t + 1) * C_GROUP)
        for col in range(n_cols):
            piece = jnp.concatenate(
                [yg_ref[col * groups_per_vreg + j, :, cols] for j in range(groups_per_vreg)], axis=1)
            y_ref[col, pl.ds(t, n_rows, stride=S5_CHUNK), :] = piece
    y = jnp.concatenate([y_ref[col] for col in range(n_cols)], axis=1)
    z = jnp.dot(y.astype(BF16), w_ref[0], preferred_element_type=F32) + b_ref[...]
    o_ref[...] = (y * jax.nn.sigmoid(z)).astype(o_ref.dtype)


def _s5_glu(y_g, glu_w_stack, o, glu_b, *, tm):
    g, n_chunks, tile = y_g.shape
    width = g * C_GROUP
    m = n_chunks * S5_CHUNK
    return pl.pallas_call(
        _s5_glu_kernel,
        grid=(m // tm,),
        in_specs=[pl.BlockSpec((g, tm // S5_CHUNK, tile), lambda i: (0, i, 0)),
                  pl.BlockSpec((1, width, width), lambda i: (o, 0, 0)),
                  pl.BlockSpec((1, width), lambda i: (0, 0))],
        out_specs=pl.BlockSpec((tm, width), lambda i: (i, 0)),
        out_shape=jax.ShapeDtypeStruct((m, width), BF16),
        scratch_shapes=[pltpu.VMEM((width // LANES, tm, LANES), F32)],
        compiler_params=_params("parallel"),
        name="s5_ungroup_glu",
    )(y_g, glu_w_stack, glu_b.reshape(1, width))


def _s5_mixer(proj, state0, a_re, a_im, log_dt, b, c, d_skip, *, batch, seq, dec_batch, dec_seq):
    nd, g, p = a_re.shape
    h = b.shape[-1]
    lc = S5_CHUNK
    m_w, p_w, q_w, a_l = _s5_chunk_weights(a_re, a_im, log_dt, b, c)
    u_g = _s5_regroup(proj, width=g * h, rows_per_step=lc * lc)
    d_l = jnp.tile(d_skip.reshape(g, 1, h), (1, 1, lc))
    h0 = state0.transpose(3, 0, 2, 1, 4).reshape(g, dec_batch, 4 * p)
    y_g, hf = _s5_chunks(u_g, m_w, p_w, q_w, a_l, d_l, h0, p_seqs=batch, p_steps=seq // lc,
                         s_seqs=dec_batch, s_steps=dec_seq // lc)
    state = hf.reshape(g, batch, 2, 2, p).transpose(1, 3, 2, 0, 4)
    return y_g, state


def _final_norm_kernel(x_ref, g_ref, o_ref):
    o_ref[...] = _rms(x_ref[...]) * g_ref[...]


def _final_norm(x, g, *, row0, n_rows, tm):
    d = x.shape[1]
    b0 = row0 // tm
    return pl.pallas_call(
        _final_norm_kernel,
        grid=(n_rows // tm,),
        in_specs=[pl.BlockSpec((tm, d), lambda i: (b0 + i, 0)),
                  pl.BlockSpec((1, d), lambda i: (0, 0))],
        out_specs=pl.BlockSpec((tm, d), lambda i: (i, 0)),
        out_shape=jax.ShapeDtypeStruct((n_rows, d), F32),
        compiler_params=_params("parallel"),
        name="final_rms_norm",
    )(x, g.reshape(1, d))


def kernel(x_prompt, x_sample, c, cache_a_k, cache_a_v, cache_d_k, cache_d_v, state_c_ssm, c_ctx,
           ada_w, ada_b, norm_g, mlp_w1, mlp_w2, even_w_in, even_w_out, diff_lambda, diff_subln,
           conv_w, conv_b, conv_ln, odd_w_in, odd_w_out, ssm_a_re, ssm_a_im, ssm_log_dt, ssm_b,
           ssm_c, ssm_d, ssm_glu_w, ssm_glu_b, qk_norm, final_norm):
    batch, seq, d_model = x_prompt.shape
    dec_batch, dec_seq, _ = x_sample.shape
    depth = ada_w.shape[0]
    n_even = even_w_in.shape[0]
    n_odd = odd_w_in.shape[0]
    n_p = batch * seq
    n_s = dec_batch * dec_seq
    n_rows = n_p + n_s
    rows_kw = dict(n_prompt_rows=n_p, dec_seq=dec_seq)
    a_qk = A_HEADS * 2 * A_DK
    a_width = A_HEADS * A_DV
    c_width = ssm_d.shape[1]
    assert dec_seq % seq == 0 and 1 + dec_batch <= SUBLANES

    x = jnp.concatenate([x_prompt.reshape(n_p, d_model), x_sample.reshape(n_s, d_model)], axis=0)
    cond8 = jnp.concatenate([c_ctx[None, :], c,
                             jnp.zeros((SUBLANES - 1 - dec_batch, d_model), F32)], axis=0)
    mods = _ada(cond8, ada_w, ada_b).reshape(depth, SUBLANES, 1, N_MOD * d_model)

    rope_a = _rope_tables(dec_seq, A_DK, 2)
    rope_d = _rope_tables(dec_seq, D_HEAD_DIM, 1)
    def layer_weights(l):
        if l % 2 == 0:
            mixer = [(even_w_in, l // 2), (even_w_out, l // 2)]
        else:
            mixer = [(odd_w_in, l // 2), (odd_w_out, l // 2), (ssm_glu_w, l // 2)]
        return mixer + [(mlp_w1, l), (mlp_w2, l)]

    in_b = even_w_in[:1].astype(BF16)
    rest_b = None

    a_kv, d_kv, c_st = None, None, []
    for l in range(depth):
        if l % 2 == 0:
            e = l // 2
            lam_init = 0.8 - 0.6 * math.exp(-0.3 * l)
            if l == 0:
                proj, rest_b = _norm_mm(x, norm_g[l, 0], mods, l, 0, in_b, 0, relu2=False,
                                        out_dtype=F32, tm=1024, tn=512, cast=layer_weights(0)[1:],
                                        **rows_kw)
            else:
                proj, _ = _norm_mm(x, norm_g[l, 0], mods, l, 0, in_b, 0, relu2=False,
                                   out_dtype=F32, tm=1024, tn=1024, **rows_kw)
            even_out_b, w1_b, w2_b = rest_b
            att, a_k, a_v = _dattn_prompt(proj, diff_lambda[e], diff_subln[e], lam_init, e, n_even,
                                          a_kv, batch=batch, seq=seq, n_rows=n_rows)
            a_kv = (a_k, a_v)
            att = _dattn_sample(proj, cache_a_k, cache_a_v, e, rope_a, diff_lambda[e],
                                diff_subln[e], lam_init, att, dec_batch=dec_batch, dec_seq=dec_seq,
                                n_prompt_rows=n_p)
            conv = _conv(proj, conv_w[e], conv_b[e], conv_ln[e], col0=2 * a_qk + a_width,
                         n_prompt_rows=n_p, prompt_seq=seq, dec_seq=dec_seq)
            x = _mm_res([att, conv], even_out_b, 0, x, mods, l, 2,
                        tm=1024, tn=1024, tk=a_width, **rows_kw)
        else:
            o = l // 2
            odd_out_b, glu_w_b, w1_b, w2_b = rest_b
            proj, _ = _norm_mm(x, norm_g[l, 0], mods, l, 0, in_b, 0, relu2=False,
                               out_dtype=F32, tm=1024, tn=1280, **rows_kw)
            y_g, state = _s5_mixer(proj, state_c_ssm[:, o], ssm_a_re[o], ssm_a_im[o],
                                   ssm_log_dt[o], ssm_b[o], ssm_c[o], ssm_d[o], batch=batch,
                                   seq=seq, dec_batch=dec_batch, dec_seq=dec_seq)
            ssm_out = _s5_glu(y_g, glu_w_b, 0, ssm_glu_b[o], tm=1024)
            att, d_k, d_v = _gqa_prompt(proj, qk_norm[o], o, n_odd, d_kv, batch=batch, seq=seq,
                                        n_rows=n_rows, c_width=c_width)
            d_kv = (d_k, d_v)
            att = _gqa_sample(proj, cache_d_k, cache_d_v, o, rope_d, qk_norm[o], att,
                              dec_batch=dec_batch, dec_seq=dec_seq, n_prompt_rows=n_p,
                              c_width=c_width)
            x = _mm_res([ssm_out, att], odd_out_b, 0, x, mods, l, 2,
                        tm=1024, tn=1024, tk=c_width, **rows_kw)
            c_st.append(state)
        ride_along = layer_weights(l + 1) if l + 1 < depth else []
        hidden, next_b = _norm_mm(x, norm_g[l, 1], mods, l, 3, w1_b, 0, relu2=True,
                                  out_dtype=BF16, tm=1024, tn=1024, cast=ride_along, **rows_kw)
        x = _mm_res([hidden], w2_b, 0, x, mods, l, 5, tm=1024, tn=1024, tk=2048, **rows_kw)
        if next_b:
            in_b, rest_b = next_b[0], next_b[1:]

    y_prompt = _final_norm(x, final_norm, row0=0, n_rows=n_p, tm=512).reshape(batch, seq, d_model)
    y_sample = _final_norm(x, final_norm, row0=n_p, n_rows=n_s, tm=512)
    y_sample = y_sample.reshape(dec_batch, dec_seq, d_model)
    return (y_prompt, y_sample, a_kv[0], a_kv[1], d_kv[0], d_kv[1], jnp.stack(c_st, axis=1))
```

```python
import functools
import math

import jax
import jax.numpy as jnp
from jax import lax
from jax.experimental import pallas as pl
from jax.experimental.pallas import tpu as pltpu

F32 = jnp.float32
BF16 = jnp.bfloat16

EPS = 1e-6
ROPE_BASE = 10000.0
GRID_W = 64
N_MOD = 6
A_HEADS = 8
A_DK = 64
A_DV = 2 * A_DK
B_KERNEL = 31
B_PAD = (B_KERNEL - 1) // 2
C_GROUP = 16
C_STATE = 64
D_HEADS = 8
D_KV_HEADS = 2
D_REP = D_HEADS // D_KV_HEADS
D_HEAD_DIM = 128

LANES = 128
SUBLANES = 8
VMEM_LIMIT = 56 * 1024 * 1024
NORM_ROWS = 16
CAST_BLOCKS = 64

S5_CHUNK = 16
S5_TILE = S5_CHUNK * C_GROUP
S5_GROUPS_PER_STEP = 4
S5_LAG_GROUPS_PER_STEP = 8


def _params(*sem):
    return pltpu.CompilerParams(dimension_semantics=sem, vmem_limit_bytes=VMEM_LIMIT)


def _mod_row(row0, n_prompt_rows, dec_seq):
    return jnp.maximum(row0 + (dec_seq - n_prompt_rows), 0) // dec_seq


def _ada_kernel(c_ref, w_ref, b_ref, o_ref):
    c = c_ref[...]
    s = (c * jax.nn.sigmoid(c)).astype(BF16)
    o_ref[0] = jnp.dot(s, w_ref[0].astype(BF16), preferred_element_type=F32) + b_ref[0]


def _ada(cond8, ada_w, ada_b):
    depth, d, n = ada_w.shape
    tn = 1024
    return pl.pallas_call(
        _ada_kernel,
        grid=(depth, n // tn),
        in_specs=[pl.BlockSpec((SUBLANES, d), lambda l, j: (0, 0)),
                  pl.BlockSpec((1, d, tn), lambda l, j: (l, 0, j)),
                  pl.BlockSpec((1, 1, tn), lambda l, j: (l, 0, j))],
        out_specs=pl.BlockSpec((1, SUBLANES, tn), lambda l, j: (l, 0, j)),
        out_shape=jax.ShapeDtypeStruct((depth, SUBLANES, n), F32),
        compiler_params=_params("parallel", "parallel"),
        name="ada_modulation",
    )(cond8, ada_w, ada_b.reshape(depth, 1, n))


def _rms(x):
    return x * lax.rsqrt(jnp.mean(x * x, axis=-1, keepdims=True) + EPS)


def _norm_mm_kernel(x_ref, g_ref, sh_ref, sc_ref, w_ref, *refs, relu2, n_cast):
    cast_in, o_ref, cast_out, h_ref = refs[:n_cast], refs[n_cast], refs[n_cast + 1:-1], refs[-1]

    @pl.when(pl.program_id(1) == 0)
    def _():
        gain = g_ref[...] * (1.0 + sc_ref[0, 0])
        shift = sh_ref[0, 0]

        def rows_body(r, carry):
            rows = pl.ds(pl.multiple_of(r * NORM_ROWS, NORM_ROWS), NORM_ROWS)
            h_ref[rows, :] = (_rms(x_ref[rows, :]) * gain + shift).astype(BF16)
            return carry

        lax.fori_loop(0, x_ref.shape[0] // NORM_ROWS, rows_body, 0, unroll=4)

    acc = jnp.dot(h_ref[...], w_ref[0], preferred_element_type=F32)
    if relu2:
        acc = jnp.maximum(acc, 0.0)
        acc = acc * acc
    o_ref[...] = acc.astype(o_ref.dtype)
    for src_ref, dst_ref in zip(cast_in, cast_out):
        dst_ref[...] = src_ref[0].astype(dst_ref.dtype)


def _norm_mm(x, g, mods, layer, chunk, w_stack, w_layer, *, relu2, out_dtype, tm, tn,
             n_prompt_rows, dec_seq, cast=()):
    m, d = x.shape
    n = w_stack.shape[2]
    ni, nj = m // tm, n // tn
    n_blk = min(CAST_BLOCKS, 1 << ((ni * nj).bit_length() - 1))
    row = lambda i: _mod_row(i * tm, n_prompt_rows, dec_seq)
    blk = lambda i, j: jnp.minimum(i * nj + j, n_blk - 1)
    cast_in_specs, cast_out_specs, cast_shapes = [], [], []
    for stack, cl in cast:
        _, r, c = stack.shape
        cast_in_specs.append(pl.BlockSpec((1, r // n_blk, c), lambda i, j, cl=cl: (cl, blk(i, j), 0)))
        cast_out_specs.append(pl.BlockSpec((r // n_blk, c), lambda i, j: (blk(i, j), 0)))
        cast_shapes.append(jax.ShapeDtypeStruct((r, c), BF16))
    outs = pl.pallas_call(
        functools.partial(_norm_mm_kernel, relu2=relu2, n_cast=len(cast)),
        grid=(ni, nj),
        in_specs=[pl.BlockSpec((tm, d), lambda i, j: (i, 0)),
                  pl.BlockSpec((1, d), lambda i, j: (0, 0)),
                  pl.BlockSpec((1, 1, 1, d), lambda i, j: (layer, row(i), 0, chunk)),
                  pl.BlockSpec((1, 1, 1, d), lambda i, j: (layer, row(i), 0, chunk + 1)),
                  pl.BlockSpec((1, d, tn), lambda i, j: (w_layer, 0, j))] + cast_in_specs,
        out_specs=[pl.BlockSpec((tm, tn), lambda i, j: (i, j))] + cast_out_specs,
        out_shape=[jax.ShapeDtypeStruct((m, n), out_dtype)] + cast_shapes,
        scratch_shapes=[pltpu.VMEM((tm, d), BF16)],
        compiler_params=_params("arbitrary", "arbitrary"),
        name="norm_mod_matmul",
    )(x, g.reshape(1, d), mods, mods, w_stack, *[stack for stack, _ in cast])
    return outs[0], [w[None] for w in outs[1:]]


def _mm_res_kernel(*refs, n_parts, nk):
    a_refs = refs[:n_parts]
    w_refs = refs[n_parts:2 * n_parts]
    res_ref, gate_ref, o_ref = refs[2 * n_parts:2 * n_parts + 3]
    part = jnp.dot(a_refs[0][...], w_refs[0][0], preferred_element_type=F32)
    for a_ref, w_ref in zip(a_refs[1:], w_refs[1:]):
        part = part + jnp.dot(a_ref[...], w_ref[0], preferred_element_type=F32)
    if nk == 1:
        o_ref[...] = res_ref[...] + gate_ref[0, 0] * part
        return
    acc_ref = refs[-1]
    k = pl.program_id(2)

    @pl.when(k == 0)
    def _():
        acc_ref[...] = part

    @pl.when(k > 0)
    def _():
        acc_ref[...] += part

    @pl.when(k == nk - 1)
    def _():
        o_ref[...] = res_ref[...] + gate_ref[0, 0] * acc_ref[...]


def _mm_res(a_parts, w_stack, w_layer, res, mods, layer, chunk, *, tm, tn, tk, n_prompt_rows,
            dec_seq):
    m, n = res.shape
    n_parts = len(a_parts)
    kdim = a_parts[0].shape[1]
    nk = kdim // tk
    gate_blk = chunk * (n // tn)
    row = lambda i: _mod_row(i * tm, n_prompt_rows, dec_seq)
    w_spec = lambda p: pl.BlockSpec((1, tk, tn), lambda i, j, k: (w_layer, p * nk + k, j))
    in_specs = ([pl.BlockSpec((tm, tk), lambda i, j, k: (i, k)) for _ in a_parts]
                + [w_spec(p) for p in range(n_parts)]
                + [pl.BlockSpec((tm, tn), lambda i, j, k: (i, j)),
                   pl.BlockSpec((1, 1, 1, tn), lambda i, j, k: (layer, row(i), 0, gate_blk + j))])
    scratch = [pltpu.VMEM((tm, tn), F32)] if nk > 1 else []
    return pl.pallas_call(
        functools.partial(_mm_res_kernel, n_parts=n_parts, nk=nk),
        grid=(m // tm, n // tn, nk),
        in_specs=in_specs,
        out_specs=pl.BlockSpec((tm, tn), lambda i, j, k: (i, j)),
        out_shape=jax.ShapeDtypeStruct((m, n), F32),
        scratch_shapes=scratch,
        compiler_params=_params("parallel", "parallel", "arbitrary"),
        name="matmul_gated_residual",
    )(*a_parts, *([w_stack] * n_parts), res, mods)


def _softmax(s):
    e = jnp.exp(s - jnp.max(s, axis=-1, keepdims=True))
    return e / jnp.sum(e, axis=-1, keepdims=True)


_QK_DIMS = (((1,), (1,)), ((), ()))


def _rope(x, cos, sin_up, sin_dn, quarter):
    width = x.shape[-1]
    return (x * cos + pltpu.roll(x, width - quarter, axis=1) * sin_up
            + pltpu.roll(x, quarter, axis=1) * sin_dn)


def _rope_tables(n_tokens, dim, reps):
    rows = n_tokens // GRID_W
    row = jnp.repeat(jnp.arange(rows, dtype=F32), GRID_W)
    col = jnp.tile(jnp.arange(GRID_W, dtype=F32), rows)
    quarter = dim // 4
    inv_freq = ROPE_BASE ** (-jnp.arange(quarter, dtype=F32) / quarter)
    ang_r = row[:, None] * inv_freq[None, :]
    ang_c = col[:, None] * inv_freq[None, :]
    ang = jnp.concatenate([ang_r, ang_r, ang_c, ang_c], axis=-1)
    cos, sin = jnp.cos(ang), jnp.sin(ang)
    even_chunk = ((jnp.arange(dim) // quarter) % 2 == 0)[None, :]
    sin_up = jnp.where(even_chunk, -sin, 0.0)
    sin_dn = jnp.where(even_chunk, 0.0, sin)
    tile = lambda t: jnp.tile(t, (1, reps))
    return tile(cos), tile(sin_up), tile(sin_dn)


def _diff_lambda(lam_ref, lam_init):
    lp = lam_ref[...]
    a = jnp.sum(lp[0:1] * lp[1:2], axis=-1, keepdims=True)
    b = jnp.sum(lp[2:3] * lp[3:4], axis=-1, keepdims=True)
    return jnp.exp(a) - jnp.exp(b) + lam_init


def _diff_attend(q, kb, vb, lam, subln, lam_init):
    lane = lax.broadcasted_iota(jnp.int32, q.shape, 1)
    q1 = jnp.where(lane < A_DK, q, 0.0).astype(BF16)
    q2 = jnp.where(lane >= A_DK, q, 0.0).astype(BF16)
    scale = A_DK ** -0.5
    s1 = lax.dot_general(q1, kb, _QK_DIMS, preferred_element_type=F32) * scale
    s2 = lax.dot_general(q2, kb, _QK_DIMS, preferred_element_type=F32) * scale
    p = _softmax(s1) - lam * _softmax(s2)
    o = jnp.dot(p.astype(BF16), vb, preferred_element_type=F32)
    return _rms(o) * subln * (1.0 - lam_init)


def _dattn_prompt_kernel(q_ref, k_ref, v_ref, lam_ref, sub_ref, *refs, lam_init):
    o_ref, ko_ref, vo_ref = refs[-3:]
    lam = _diff_lambda(lam_ref, lam_init)
    for h in range(A_HEADS):
        cols = slice(h * A_DV, (h + 1) * A_DV)
        k = k_ref[:, cols]
        v = v_ref[:, cols]
        ko_ref[0, 0, h] = k
        vo_ref[0, 0, h] = v
        o = _diff_attend(q_ref[:, cols], k.astype(BF16), v.astype(BF16), lam, sub_ref[...],
                         lam_init)
        o_ref[:, cols] = o.astype(o_ref.dtype)


def _dattn_prompt(proj, lam_p, subln, lam_init, e, n_even, kv_prev, *, batch, seq, n_rows):
    hd = 2 * A_DK
    width = A_HEADS * hd
    kv_shape = jax.ShapeDtypeStruct((batch, n_even, A_HEADS, seq, hd), F32)
    kv_spec = pl.BlockSpec((1, 1, A_HEADS, seq, hd), lambda b: (b, e, 0, 0, 0))
    extra, extra_specs, aliases = [], [], {}
    if kv_prev is not None:
        extra = list(kv_prev)
        extra_specs = [pl.BlockSpec(memory_space=pl.ANY)] * 2
        aliases = {5: 1, 6: 2}
    return pl.pallas_call(
        functools.partial(_dattn_prompt_kernel, lam_init=lam_init),
        grid=(batch,),
        in_specs=[pl.BlockSpec((seq, width), lambda b: (b, 0)),
                  pl.BlockSpec((seq, width), lambda b: (b, 1)),
                  pl.BlockSpec((seq, width), lambda b: (b, 2)),
                  pl.BlockSpec(lam_p.shape, lambda b: (0, 0)),
                  pl.BlockSpec((1, hd), lambda b: (0, 0))] + extra_specs,
        out_specs=[pl.BlockSpec((seq, width), lambda b: (b, 0)), kv_spec, kv_spec],
        out_shape=[jax.ShapeDtypeStruct((n_rows, width), BF16), kv_shape, kv_shape],
        input_output_aliases=aliases,
        compiler_params=_params("parallel"),
        name="diff_attention_prompt",
    )(proj, proj, proj, lam_p, subln.reshape(1, hd), *extra)


def _dattn_sample_kernel(q_ref, k_ref, v_ref, ck_ref, cv_ref, cos_ref, su_ref, sd_ref, lam_ref,
                         sub_ref, att_in_ref, o_ref, kf_ref, vf_ref, *, lam_init, tq):
    del att_in_ref
    t = k_ref.shape[0]
    quarter = A_DK // 4
    kf_ref[0:t] = _rope(k_ref[...], cos_ref[...], su_ref[...], sd_ref[...], quarter).astype(BF16)
    kf_ref[t:] = ck_ref[0, 0, 0].astype(BF16)
    vf_ref[0:t] = v_ref[...].astype(BF16)
    vf_ref[t:] = cv_ref[0, 0, 0].astype(BF16)
    lam = _diff_lambda(lam_ref, lam_init)
    for i in range(t // tq):
        rows = slice(i * tq, (i + 1) * tq)
        q = _rope(q_ref[rows, :], cos_ref[rows, :], su_ref[rows, :], sd_ref[rows, :], quarter)
        o = _diff_attend(q, kf_ref[...], vf_ref[...], lam, sub_ref[...], lam_init)
        o_ref[rows, :] = o.astype(o_ref.dtype)


def _dattn_sample(proj, cache_k, cache_v, e, rope, lam_p, subln, lam_init, att, *,
                  dec_batch, dec_seq, n_prompt_rows):
    hd = 2 * A_DK
    past = cache_k.shape[3]
    rb = n_prompt_rows // dec_seq
    cache_spec = pl.BlockSpec((1, 1, 1, past, hd), lambda b, h: (b, e, h, 0, 0))
    tab_spec = pl.BlockSpec((dec_seq, hd), lambda b, h: (0, 0))
    return pl.pallas_call(
        functools.partial(_dattn_sample_kernel, lam_init=lam_init, tq=256),
        grid=(dec_batch, A_HEADS),
        in_specs=[pl.BlockSpec((dec_seq, hd), lambda b, h: (rb + b, h)),
                  pl.BlockSpec((dec_seq, hd), lambda b, h: (rb + b, A_HEADS + h)),
                  pl.BlockSpec((dec_seq, hd), lambda b, h: (rb + b, 2 * A_HEADS + h)),
                  cache_spec, cache_spec, tab_spec, tab_spec, tab_spec,
                  pl.BlockSpec(lam_p.shape, lambda b, h: (0, 0)),
                  pl.BlockSpec((1, hd), lambda b, h: (0, 0)),
                  pl.BlockSpec(memory_space=pl.ANY)],
        out_specs=pl.BlockSpec((dec_seq, hd), lambda b, h: (rb + b, h)),
        out_shape=jax.ShapeDtypeStruct(att.shape, att.dtype),
        scratch_shapes=[pltpu.VMEM((dec_seq + past, hd), BF16),
                        pltpu.VMEM((dec_seq + past, hd), BF16)],
        input_output_aliases={10: 0},
        compiler_params=_params("parallel", "parallel"),
        name="diff_attention_sample",
    )(proj, proj, proj, cache_k, cache_v, *rope, lam_p, subln.reshape(1, hd), att)


def _gqa_attend(q, kb, vb):
    s = lax.dot_general(q.astype(BF16), kb, _QK_DIMS, preferred_element_type=F32) * (D_HEAD_DIM ** -0.5)
    return jnp.dot(_softmax(s).astype(BF16), vb, preferred_element_type=F32)


def _gqa_prompt_kernel(q_ref, k_ref, v_ref, g_ref, *refs):
    o_ref, ko_ref, vo_ref = refs[-3:]
    gq = g_ref[0:1]
    gk = g_ref[1:2]
    for g in range(D_KV_HEADS):
        kv_cols = slice(g * D_HEAD_DIM, (g + 1) * D_HEAD_DIM)
        k = _rms(k_ref[:, kv_cols]) * gk
        v = v_ref[:, kv_cols]
        ko_ref[0, 0, g] = k
        vo_ref[0, 0, g] = v
        kb = k.astype(BF16)
        vb = v.astype(BF16)
        for r in range(D_REP):
            head = g * D_REP + r
            cols = slice(head * D_HEAD_DIM, (head + 1) * D_HEAD_DIM)
            q = _rms(q_ref[:, cols]) * gq
            o_ref[:, cols] = _gqa_attend(q, kb, vb).astype(o_ref.dtype)


def _gqa_prompt(proj, qk_g, o, n_odd, kv_prev, *, batch, seq, n_rows, c_width):
    hd = D_HEAD_DIM
    qw = D_HEADS * hd
    kvw = D_KV_HEADS * hd
    q0 = c_width // qw
    k0 = (c_width + qw) // kvw
    kv_shape = jax.ShapeDtypeStruct((batch, n_odd, D_KV_HEADS, seq, hd), F32)
    kv_spec = pl.BlockSpec((1, 1, D_KV_HEADS, seq, hd), lambda b: (b, o, 0, 0, 0))
    extra, extra_specs, aliases = [], [], {}
    if kv_prev is not None:
        extra = list(kv_prev)
        extra_specs = [pl.BlockSpec(memory_space=pl.ANY)] * 2
        aliases = {4: 1, 5: 2}
    return pl.pallas_call(
        _gqa_prompt_kernel,
        grid=(batch,),
        in_specs=[pl.BlockSpec((seq, qw), lambda b: (b, q0)),
                  pl.BlockSpec((seq, kvw), lambda b: (b, k0)),
                  pl.BlockSpec((seq, kvw), lambda b: (b, k0 + 1)),
                  pl.BlockSpec(qk_g.shape, lambda b: (0, 0))] + extra_specs,
        out_specs=[pl.BlockSpec((seq, qw), lambda b: (b, 0)), kv_spec, kv_spec],
        out_shape=[jax.ShapeDtypeStruct((n_rows, qw), BF16), kv_shape, kv_shape],
        input_output_aliases=aliases,
        compiler_params=_params("parallel"),
        name="gqa_prompt",
    )(proj, proj, proj, qk_g, *extra)


def _gqa_sample_kernel(q_ref, k_ref, v_ref, ck_ref, cv_ref, cos_ref, su_ref, sd_ref, g_ref,
                       att_in_ref, o_ref, kf_ref, vf_ref, *, tq):
    del att_in_ref
    t = k_ref.shape[0]
    quarter = D_HEAD_DIM // 4
    gq = g_ref[0:1]
    k = _rms(k_ref[...]) * g_ref[1:2]
    kf_ref[0:t] = _rope(k, cos_ref[...], su_ref[...], sd_ref[...], quarter).astype(BF16)
    kf_ref[t:] = ck_ref[0, 0, 0].astype(BF16)
    vf_ref[0:t] = v_ref[...].astype(BF16)
    vf_ref[t:] = cv_ref[0, 0, 0].astype(BF16)
    for r in range(D_REP):
        cols = slice(r * D_HEAD_DIM, (r + 1) * D_HEAD_DIM)
        for i in range(t // tq):
            rows = slice(i * tq, (i + 1) * tq)
            q = _rms(q_ref[rows, cols]) * gq
            q = _rope(q, cos_ref[rows, :], su_ref[rows, :], sd_ref[rows, :], quarter)
            o_ref[rows, cols] = _gqa_attend(q, kf_ref[...], vf_ref[...]).astype(o_ref.dtype)


def _gqa_sample(proj, cache_k, cache_v, o, rope, qk_g, att, *, dec_batch, dec_seq, n_prompt_rows,
                c_width):
    hd = D_HEAD_DIM
    qw = D_REP * hd
    q0 = c_width // qw
    k0 = (c_width + D_HEADS * hd) // hd
    v0 = k0 + D_KV_HEADS
    past = cache_k.shape[3]
    rb = n_prompt_rows // dec_seq
    cache_spec = pl.BlockSpec((1, 1, 1, past, hd), lambda b, g: (b, o, g, 0, 0))
    tab_spec = pl.BlockSpec((dec_seq, hd), lambda b, g: (0, 0))
    return pl.pallas_call(
        functools.partial(_gqa_sample_kernel, tq=256),
        grid=(dec_batch, D_KV_HEADS),
        in_specs=[pl.BlockSpec((dec_seq, qw), lambda b, g: (rb + b, q0 + g)),
                  pl.BlockSpec((dec_seq, hd), lambda b, g: (rb + b, k0 + g)),
                  pl.BlockSpec((dec_seq, hd), lambda b, g: (rb + b, v0 + g)),
                  cache_spec, cache_spec, tab_spec, tab_spec, tab_spec,
                  pl.BlockSpec(qk_g.shape, lambda b, g: (0, 0)),
                  pl.BlockSpec(memory_space=pl.ANY)],
        out_specs=pl.BlockSpec((dec_seq, qw), lambda b, g: (rb + b, g)),
        out_shape=jax.ShapeDtypeStruct(att.shape, att.dtype),
        scratch_shapes=[pltpu.VMEM((dec_seq + past, hd), BF16),
                        pltpu.VMEM((dec_seq + past, hd), BF16)],
        input_output_aliases={9: 0},
        compiler_params=_params("parallel", "parallel"),
        name="gqa_sample",
    )(proj, proj, proj, cache_k, cache_v, *rope, qk_g, att)


CONV_ROWS = 256
CONV_HALO = 16


def _conv_kernel(a_ref, g_ref, ap_ref, gp_ref, an_ref, gn_ref, w_ref, b_ref, lng_ref, lnb_ref,
                 o_ref, pad_ref, cv_ref, *, n_prompt_blocks, blocks_per_seq):
    rows, width = a_ref.shape
    n_chunks = width // LANES
    i = pl.program_id(0)
    j = jnp.maximum(i - n_prompt_blocks, 0) % blocks_per_seq
    latent = i >= n_prompt_blocks
    has_prev = jnp.where(jnp.logical_and(latent, j > 0), 1.0, 0.0)
    has_next = jnp.where(jnp.logical_and(latent, j < blocks_per_seq - 1), 1.0, 0.0)

    glu = lambda a, g: a * jax.nn.sigmoid(g)
    h_prev = glu(ap_ref[...], gp_ref[...]) * has_prev
    h_cur = glu(a_ref[...], g_ref[...])
    h_next = glu(an_ref[...], gn_ref[...]) * has_next
    for c in range(n_chunks):
        lanes = slice(c * LANES, (c + 1) * LANES)
        pad_ref[c, 0:CONV_HALO, :] = h_prev[:, lanes]
        pad_ref[c, CONV_HALO:CONV_HALO + rows, :] = h_cur[:, lanes]
        pad_ref[c, CONV_HALO + rows:, :] = h_next[:, lanes]

    def chunk_body(c, carry):
        for r in range(rows // SUBLANES):
            base = CONV_HALO - B_PAD + r * SUBLANES
            acc = w_ref[c, 0] * pad_ref[c, pl.ds(base, SUBLANES), :]
            for tap in range(1, B_KERNEL):
                acc = acc + w_ref[c, tap] * pad_ref[c, pl.ds(base + tap, SUBLANES), :]
            cv_ref[c, pl.ds(r * SUBLANES, SUBLANES), :] = acc
        return carry

    lax.fori_loop(0, n_chunks, chunk_body, 0)

    x = jnp.concatenate([cv_ref[c] for c in range(n_chunks)], axis=1) + b_ref[...]
    mu = jnp.mean(x, axis=-1, keepdims=True)
    xc = x - mu
    var = jnp.mean(xc * xc, axis=-1, keepdims=True)
    y = xc * lax.rsqrt(var + EPS) * lng_ref[...] + lnb_ref[...]
    o_ref[...] = (y * jax.nn.sigmoid(y)).astype(o_ref.dtype)


def _conv(proj, conv_w, conv_b, conv_ln, *, col0, n_prompt_rows, prompt_seq, dec_seq):
    m = proj.shape[0]
    taps, width = conv_w.shape
    assert prompt_seq == CONV_ROWS and dec_seq % CONV_ROWS == 0
    n_chunks = width // LANES
    a_blk = col0 // width
    halo_per_blk = CONV_ROWS // CONV_HALO
    last_halo = m // CONV_HALO - 1
    prev_map = lambda cb: (lambda i: (jnp.maximum(i * halo_per_blk - 1, 0), cb))
    next_map = lambda cb: (lambda i: (jnp.minimum((i + 1) * halo_per_blk, last_halo), cb))
    w8 = jnp.broadcast_to(conv_w.reshape(taps, 1, n_chunks, LANES), (taps, SUBLANES, n_chunks, LANES))
    w8 = w8.transpose(2, 0, 1, 3)
    row_spec = lambda: pl.BlockSpec((1, width), lambda i: (0, 0))
    return pl.pallas_call(
        functools.partial(_conv_kernel, n_prompt_blocks=n_prompt_rows // CONV_ROWS,
                          blocks_per_seq=dec_seq // CONV_ROWS),
        grid=(m // CONV_ROWS,),
        in_specs=[pl.BlockSpec((CONV_ROWS, width), lambda i: (i, a_blk)),
                  pl.BlockSpec((CONV_ROWS, width), lambda i: (i, a_blk + 1)),
                  pl.BlockSpec((CONV_HALO, width), prev_map(a_blk)),
                  pl.BlockSpec((CONV_HALO, width), prev_map(a_blk + 1)),
                  pl.BlockSpec((CONV_HALO, width), next_map(a_blk)),
                  pl.BlockSpec((CONV_HALO, width), next_map(a_blk + 1)),
                  pl.BlockSpec(w8.shape, lambda i: (0, 0, 0, 0)),
                  row_spec(), row_spec(), row_spec()],
        out_specs=pl.BlockSpec((CONV_ROWS, width), lambda i: (i, 0)),
        out_shape=jax.ShapeDtypeStruct((m, width), BF16),
        scratch_shapes=[pltpu.VMEM((n_chunks, CONV_ROWS + 2 * CONV_HALO, LANES), F32),
                        pltpu.VMEM((n_chunks, CONV_ROWS, LANES), F32)],
        compiler_params=_params("parallel"),
        name="conformer_conv",
    )(proj, proj, proj, proj, proj, proj, w8, conv_b.reshape(1, width),
      conv_ln[0].reshape(1, width), conv_ln[1].reshape(1, width))


def _s5_weights_kernel(ar_ref, ai_ref, ldt_ref, btr_ref, bti_ref, cr_ref, ci_ref,
                       mt_ref, p_ref, qt_ref, al_ref):
    n_lag = S5_CHUNK
    hi = lax.Precision.HIGHEST
    disc = []
    for d in range(2):
        ar = ar_ref[d]
        ai = ai_ref[d]
        dt = jnp.exp(ldt_ref[d])
        mag = jnp.exp(ar * dt)
        abr = mag * jnp.cos(ai * dt)
        abi = mag * jnp.sin(ai * dt)
        den = ar * ar + ai * ai
        nr = abr - 1.0
        ni = abi
        disc.append((abr, abi, (nr * ar + ni * ai) / den, (ni * ar - nr * ai) / den))
    for gi in range(ar_ref.shape[1]):
        row = slice(gi, gi + 1)
        tables = []
        for d in range(2):
            abr, abi, kr, ki = disc[d]
            a_r, a_i = abr[row], abi[row]
            bbr = kr[row] * btr_ref[d, gi] - ki[row] * bti_ref[d, gi]
            bbi = kr[row] * bti_ref[d, gi] + ki[row] * btr_ref[d, gi]
            cr = cr_ref[d, gi]
            ci = ci_ref[d, gi]
            pr, pi = jnp.ones_like(a_r), jnp.zeros_like(a_r)
            wt_r, wt_i, v_r, v_i = [], [], [], []
            for lag in range(n_lag + 1):
                if lag < n_lag:
                    wt_r.append(pr * bbr - pi * bbi)
                    wt_i.append(pr * bbi + pi * bbr)
                if lag >= 1:
                    v_r.append(cr * pr - ci * pi)
                    v_i.append(cr * pi + ci * pr)
                if lag < n_lag:
                    pr, pi = pr * a_r - pi * a_i, pr * a_i + pi * a_r
            if d == 0:
                wt_r.reverse()
                wt_i.reverse()
            else:
                v_r.reverse()
                v_i.reverse()
            wtr = jnp.concatenate(wt_r, axis=0)
            wti = jnp.concatenate(wt_i, axis=0)
            klag = (lax.dot_general(cr, wtr, _QK_DIMS, precision=hi, preferred_element_type=F32)
                    - lax.dot_general(ci, wti, _QK_DIMS, precision=hi, preferred_element_type=F32))
            tables.append((wtr, wti, jnp.concatenate(v_r, axis=0), jnp.concatenate(v_i, axis=0),
                           klag, pr, pi))
        (wtr_f, wti_f, vr_f, vi_f, k_f, alr_f, ali_f), (wtr_b, wti_b, vr_b, vi_b, k_b, alr_b, ali_b) = tables
        p_ref[gi] = jnp.concatenate([wtr_f, wtr_b, wti_f, wti_b], axis=1).astype(p_ref.dtype)
        qt_ref[gi] = jnp.concatenate([vr_f, vr_b, -vi_f, -vi_b], axis=1).astype(qt_ref.dtype)
        al_ref[gi] = jnp.concatenate([alr_f, alr_b, ali_f, ali_b], axis=1)
        h = k_f.shape[0]
        blocks = []
        for t in range(n_lag):
            lo = (n_lag - 1 - t) * h
            fwd = k_f if lo == 0 else jnp.concatenate([k_f[:, lo:], jnp.zeros((h, lo), F32)], axis=1)
            bwd = k_b if t == 0 else jnp.concatenate(
                [jnp.zeros((h, t * h), F32), k_b[:, :(n_lag - t) * h]], axis=1)
            blocks.append(fwd + bwd)
        mt_ref[gi] = jnp.concatenate(blocks, axis=0).astype(mt_ref.dtype)


def _s5_chunk_weights(a_re, a_im, log_dt, b, c):
    nd, g, p = a_re.shape
    h = b.shape[-1]
    gb = S5_LAG_GROUPS_PER_STEP
    tile = S5_CHUNK * h
    bt = b.transpose(0, 1, 2, 4, 3)
    ldt = jnp.broadcast_to(log_dt[:, :, None], (nd, g, p))
    vec = pl.BlockSpec((nd, gb, p), lambda i: (0, i, 0))
    mat = pl.BlockSpec((nd, gb, h, p), lambda i: (0, i, 0, 0))
    sq = pl.BlockSpec((gb, tile, tile), lambda i: (i, 0, 0))
    sq_shape = jax.ShapeDtypeStruct((g, tile, tile), BF16)
    return pl.pallas_call(
        _s5_weights_kernel,
        grid=(g // gb,),
        in_specs=[vec, vec, vec, mat, mat, mat, mat],
        out_specs=[sq, sq, sq, pl.BlockSpec((gb, 1, 4 * p), lambda i: (i, 0, 0))],
        out_shape=[sq_shape, sq_shape, sq_shape, jax.ShapeDtypeStruct((g, 1, 4 * p), F32)],
        compiler_params=_params("parallel"),
        name="s5_chunk_weights",
    )(a_re, a_im, ldt, bt[:, 0], bt[:, 1], c[:, 0], c[:, 1])


def _s5_regroup_kernel(*refs):
    x_refs, o_ref = refs[:-1], refs[-1]
    n_chunks = x_refs[0].shape[0] // S5_CHUNK
    groups_per_vreg = LANES // C_GROUP
    for col, x_ref in enumerate(x_refs):
        xs = [x_ref[pl.ds(s, n_chunks, stride=S5_CHUNK), :] for s in range(S5_CHUNK)]
        for j in range(groups_per_vreg):
            cols = slice(j * C_GROUP, (j + 1) * C_GROUP)
            row = jnp.concatenate([x[:, cols] for x in xs], axis=1)
            o_ref[col * groups_per_vreg + j] = row.astype(o_ref.dtype)


def _s5_regroup(proj, *, width, rows_per_step):
    m = proj.shape[0]
    g = width // C_GROUP
    n_cols = width // LANES
    chunks_per_step = rows_per_step // S5_CHUNK
    col_spec = lambda c: pl.BlockSpec((rows_per_step, LANES), lambda i: (i, c))
    return pl.pallas_call(
        _s5_regroup_kernel,
        grid=(m // rows_per_step,),
        in_specs=[col_spec(c) for c in range(n_cols)],
        out_specs=pl.BlockSpec((g, chunks_per_step, S5_TILE), lambda i: (0, i, 0)),
        out_shape=jax.ShapeDtypeStruct((g, m // S5_CHUNK, S5_TILE), BF16),
        compiler_params=_params("parallel"),
        name="s5_regroup",
    )(*([proj] * n_cols))


def _s5_chunk_kernel(u_ref, m_ref, p_ref, q_ref, al_ref, d_ref, h0_ref, y_ref, hf_ref, x_ref,
                     sa_ref, sd_ref, *, p_seqs, p_steps, s_seqs, s_steps):
    half = x_ref.shape[2]
    lane = lax.broadcasted_iota(jnp.int32, (1, half), 1)
    fwd = lane < (half // 2)

    def scan(out_ref, gi, row0, seqs, steps, sr, si, reverse):
        alr = al_ref[gi, :, 0:half]
        ali = al_ref[gi, :, half:]
        for c in (range(steps - 1, -1, -1) if reverse else range(steps)):
            rs = pl.ds(row0 + c, seqs, stride=steps)
            out_ref[0, rs, :] = sr
            out_ref[1, rs, :] = si
            xr = x_ref[0, rs, :]
            xi = x_ref[1, rs, :]
            sr, si = alr * sr - ali * si + xr, alr * si + ali * sr + xi
        return sr, si

    def group(gi, carry):
        u = u_ref[gi]
        x = jnp.dot(u, p_ref[gi], preferred_element_type=F32)
        x_ref[0] = x[:, 0:half]
        x_ref[1] = x[:, half:]
        zero = jnp.zeros((p_seqs, half), F32)
        fa = scan(sa_ref, gi, 0, p_seqs, p_steps, zero, zero, False)
        fd = scan(sd_ref, gi, 0, p_seqs, p_steps, zero, zero, True)
        h0r = h0_ref[gi, :, 0:half]
        h0i = h0_ref[gi, :, half:]
        s_row0 = p_seqs * p_steps
        scan(sa_ref, gi, s_row0, s_seqs, s_steps, h0r, h0i, False)
        scan(sd_ref, gi, s_row0, s_seqs, s_steps, h0r, h0i, True)
        s = jnp.concatenate([jnp.where(fwd, sa_ref[0], sd_ref[0]),
                             jnp.where(fwd, sa_ref[1], sd_ref[1])], axis=1)
        y = (lax.dot_general(u, m_ref[gi], _QK_DIMS, preferred_element_type=F32)
             + lax.dot_general(s.astype(BF16), q_ref[gi], _QK_DIMS, preferred_element_type=F32))
        y_ref[gi] = jax.nn.gelu(d_ref[gi] * u.astype(F32) + y)
        hf_ref[gi, :, 0:half] = jnp.where(fwd, fa[0], fd[0])
        hf_ref[gi, :, half:] = jnp.where(fwd, fa[1], fd[1])
        return carry

    lax.fori_loop(0, u_ref.shape[0], group, 0)


def _s5_chunks(u_g, m_w, p_w, q_w, a_l, d_l, h0, *, p_seqs, p_steps, s_seqs, s_steps):
    g, rows, tile = u_g.shape
    gb = S5_GROUPS_PER_STEP
    blk = lambda r, c: pl.BlockSpec((gb, r, c), lambda i: (i, 0, 0))
    return pl.pallas_call(
        functools.partial(_s5_chunk_kernel, p_seqs=p_seqs, p_steps=p_steps, s_seqs=s_seqs,
                          s_steps=s_steps),
        grid=(g // gb,),
        in_specs=[blk(rows, tile), blk(tile, tile), blk(tile, tile), blk(tile, tile),
                  blk(1, tile), blk(1, tile), blk(s_seqs, tile)],
        out_specs=[blk(rows, tile), blk(p_seqs, tile)],
        out_shape=[jax.ShapeDtypeStruct((g, rows, tile), F32),
                   jax.ShapeDtypeStruct((g, p_seqs, tile), F32)],
        scratch_shapes=[pltpu.VMEM((2, rows, tile // 2), F32)] * 3,
        compiler_params=_params("parallel"),
        name="s5_chunks",
    )(u_g, m_w, p_w, q_w, a_l, d_l, h0)


def _s5_glu_kernel(yg_ref, w_ref, b_ref, o_ref, y_ref):
    n_rows = yg_ref.shape[1]
    n_cols = y_ref.shape[0]
    groups_per_vreg = LANES // C_GROUP
    for t in range(S5_CHUNK):
        cols = slice(t * C_GROUP, (t + 1) * C_GROUP)
        for col in range(n_cols):
            piece = jnp.concatenate(
                [yg_ref[col * groups_per_vreg + j, :, cols] for j in range(groups_per_vreg)], axis=1)
            y_ref[col, pl.ds(t, n_rows, stride=S5_CHUNK), :] = piece
    y = jnp.concatenate([y_ref[col] for col in range(n_cols)], axis=1)
    z = jnp.dot(y.astype(BF16), w_ref[0], preferred_element_type=F32) + b_ref[...]
    o_ref[...] = (y * jax.nn.sigmoid(z)).astype(o_ref.dtype)


def _s5_glu(y_g, glu_w_stack, o, glu_b, *, tm):
    g, n_chunks, tile = y_g.shape
    width = g * C_GROUP
    m = n_chunks * S5_CHUNK
    return pl.pallas_call(
        _s5_glu_kernel,
        grid=(m // tm,),
        in_specs=[pl.BlockSpec((g, tm // S5_CHUNK, tile), lambda i: (0, i, 0)),
                  pl.BlockSpec((1, width, width), lambda i: (o, 0, 0)),
                  pl.BlockSpec((1, width), lambda i: (0, 0))],
        out_specs=pl.BlockSpec((tm, width), lambda i: (i, 0)),
        out_shape=jax.ShapeDtypeStruct((m, width), BF16),
        scratch_shapes=[pltpu.VMEM((width // LANES, tm, LANES), F32)],
        compiler_params=_params("parallel"),
        name="s5_ungroup_glu",
    )(y_g, glu_w_stack, glu_b.reshape(1, width))


def _s5_mixer(proj, state0, a_re, a_im, log_dt, b, c, d_skip, *, batch, seq, dec_batch, dec_seq):
    nd, g, p = a_re.shape
    h = b.shape[-1]
    lc = S5_CHUNK
    m_w, p_w, q_w, a_l = _s5_chunk_weights(a_re, a_im, log_dt, b, c)
    u_g = _s5_regroup(proj, width=g * h, rows_per_step=lc * lc)
    d_l = jnp.tile(d_skip.reshape(g, 1, h), (1, 1, lc))
    h0 = state0.transpose(3, 0, 2, 1, 4).reshape(g, dec_batch, 4 * p)
    y_g, hf = _s5_chunks(u_g, m_w, p_w, q_w, a_l, d_l, h0, p_seqs=batch, p_steps=seq // lc,
                         s_seqs=dec_batch, s_steps=dec_seq // lc)
    state = hf.reshape(g, batch, 2, 2, p).transpose(1, 3, 2, 0, 4)
    return y_g, state


def _final_norm_kernel(x_ref, g_ref, o_ref):
    o_ref[...] = _rms(x_ref[...]) * g_ref[...]


def _final_norm(x, g, *, row0, n_rows, tm):
    d = x.shape[1]
    b0 = row0 // tm
    return pl.pallas_call(
        _final_norm_kernel,
        grid=(n_rows // tm,),
        in_specs=[pl.BlockSpec((tm, d), lambda i: (b0 + i, 0)),
                  pl.BlockSpec((1, d), lambda i: (0, 0))],
        out_specs=pl.BlockSpec((tm, d), lambda i: (i, 0)),
        out_shape=jax.ShapeDtypeStruct((n_rows, d), F32),
        compiler_params=_params("parallel"),
        name="final_rms_norm",
    )(x, g.reshape(1, d))


def kernel(x_prompt, x_sample, c, cache_a_k, cache_a_v, cache_d_k, cache_d_v, state_c_ssm, c_ctx,
           ada_w, ada_b, norm_g, mlp_w1, mlp_w2, even_w_in, even_w_out, diff_lambda, diff_subln,
           conv_w, conv_b, conv_ln, odd_w_in, odd_w_out, ssm_a_re, ssm_a_im, ssm_log_dt, ssm_b,
           ssm_c, ssm_d, ssm_glu_w, ssm_glu_b, qk_norm, final_norm):
    batch, seq, d_model = x_prompt.shape
    dec_batch, dec_seq, _ = x_sample.shape
    depth = ada_w.shape[0]
    n_even = even_w_in.shape[0]
    n_odd = odd_w_in.shape[0]
    n_p = batch * seq
    n_s = dec_batch * dec_seq
    n_rows = n_p + n_s
    rows_kw = dict(n_prompt_rows=n_p, dec_seq=dec_seq)
    a_qk = A_HEADS * 2 * A_DK
    a_width = A_HEADS * A_DV
    c_width = ssm_d.shape[1]
    assert dec_seq % seq == 0 and 1 + dec_batch <= SUBLANES

    x = jnp.concatenate([x_prompt.reshape(n_p, d_model), x_sample.reshape(n_s, d_model)], axis=0)
    cond8 = jnp.concatenate([c_ctx[None, :], c,
                             jnp.zeros((SUBLANES - 1 - dec_batch, d_model), F32)], axis=0)
    mods = _ada(cond8, ada_w, ada_b).reshape(depth, SUBLANES, 1, N_MOD * d_model)

    rope_a = _rope_tables(dec_seq, A_DK, 2)
    rope_d = _rope_tables(dec_seq, D_HEAD_DIM, 1)
    def layer_weights(l):
        if l % 2 == 0:
            mixer = [(even_w_in, l // 2), (even_w_out, l // 2)]
        else:
            mixer = [(odd_w_in, l // 2), (odd_w_out, l // 2), (ssm_glu_w, l // 2)]
        return mixer + [(mlp_w1, l), (mlp_w2, l)]

    in_b = even_w_in[:1].astype(BF16)
    rest_b = None

    a_kv, d_kv, c_st = None, None, []
    for l in range(depth):
        if l % 2 == 0:
            e = l // 2
            lam_init = 0.8 - 0.6 * math.exp(-0.3 * l)
            if l == 0:
                proj, rest_b = _norm_mm(x, norm_g[l, 0], mods, l, 0, in_b, 0, relu2=False,
                                        out_dtype=F32, tm=1024, tn=1024, cast=layer_weights(0)[1:],
                                        **rows_kw)
            else:
                proj, _ = _norm_mm(x, norm_g[l, 0], mods, l, 0, in_b, 0, relu2=False,
                                   out_dtype=F32, tm=1024, tn=1024, **rows_kw)
            even_out_b, w1_b, w2_b = rest_b
            att, a_k, a_v = _dattn_prompt(proj, diff_lambda[e], diff_subln[e], lam_init, e, n_even,
                                          a_kv, batch=batch, seq=seq, n_rows=n_rows)
            a_kv = (a_k, a_v)
            att = _dattn_sample(proj, cache_a_k, cache_a_v, e, rope_a, diff_lambda[e],
                                diff_subln[e], lam_init, att, dec_batch=dec_batch, dec_seq=dec_seq,
                                n_prompt_rows=n_p)
            conv = _conv(proj, conv_w[e], conv_b[e], conv_ln[e], col0=2 * a_qk + a_width,
                         n_prompt_rows=n_p, prompt_seq=seq, dec_seq=dec_seq)
            x = _mm_res([att, conv], even_out_b, 0, x, mods, l, 2,
                        tm=1024, tn=1024, tk=a_width, **rows_kw)
        else:
            o = l // 2
            odd_out_b, glu_w_b, w1_b, w2_b = rest_b
            proj, _ = _norm_mm(x, norm_g[l, 0], mods, l, 0, in_b, 0, relu2=False,
                               out_dtype=F32, tm=1024, tn=1280, **rows_kw)
            y_g, state = _s5_mixer(proj, state_c_ssm[:, o], ssm_a_re[o], ssm_a_im[o],
                                   ssm_log_dt[o], ssm_b[o], ssm_c[o], ssm_d[o], batch=batch,
                                   seq=seq, dec_batch=dec_batch, dec_seq=dec_seq)
            ssm_out = _s5_glu(y_g, glu_w_b, 0, ssm_glu_b[o], tm=1024)
            att, d_k, d_v = _gqa_prompt(proj, qk_norm[o], o, n_odd, d_kv, batch=batch, seq=seq,
                                        n_rows=n_rows, c_width=c_width)
            d_kv = (d_k, d_v)
            att = _gqa_sample(proj, cache_d_k, cache_d_v, o, rope_d, qk_norm[o], att,
                              dec_batch=dec_batch, dec_seq=dec_seq, n_prompt_rows=n_p,
                              c_width=c_width)
            x = _mm_res([ssm_out, att], odd_out_b, 0, x, mods, l, 2,
                        tm=1024, tn=1024, tk=c_width, **rows_kw)
            c_st.append(state)
        ride_along = layer_weights(l + 1) if l + 1 < depth else []
        hidden, next_b = _norm_mm(x, norm_g[l, 1], mods, l, 3, w1_b, 0, relu2=True,
                                  out_dtype=BF16, tm=1024, tn=1024, cast=ride_along, **rows_kw)
        x = _mm_res([hidden], w2_b, 0, x, mods, l, 5, tm=1024, tn=1024, tk=2048, **rows_kw)
        if next_b:
            in_b, rest_b = next_b[0], next_b[1:]

    y_prompt = _final_norm(x, final_norm, row0=0, n_rows=n_p, tm=512).reshape(batch, seq, d_model)
    y_sample = _final_norm(x, final_norm, row0=n_p, n_rows=n_s, tm=512)
    y_sample = y_sample.reshape(dec_batch, dec_seq, d_model)
    return (y_prompt, y_sample, a_kv[0], a_kv[1], d_kv[0], d_kv[1], jnp.stack(c_st, axis=1))
```
